```python
import math
import jax
import jax.numpy as jnp
from jax import lax
import numpy as np


D_MODEL = 2048
BATCH = 2
SEQ = 16384
DEPTH = 2

N_A_LAYERS = DEPTH // 2
N_B_LAYERS = DEPTH - N_A_LAYERS
Q_BLOCK = 128
ROPE_THETA = 10000.0
NORM_EPS = 1e-6

MLA_HEADS = 12
MLA_NOPE_DIM = 128
MLA_ROPE_DIM = 64
MLA_V_DIM = 128
Q_LORA_RANK = 512
KV_LORA_RANK = 512

DIFF_HEADS = 6
DIFF_HEAD_DIM = 128
DIFF_V_DIM = 2 * DIFF_HEAD_DIM

MEM_TOKENS = 256
MEM_HEADS = 4
MEM_HEAD_DIM = 128

D_FF = 5632
CONV_WIDTH = 3

A_IN_WIDTH = Q_LORA_RANK + KV_LORA_RANK + MLA_ROPE_DIM + MEM_HEADS * MEM_HEAD_DIM
B_IN_WIDTH = DIFF_HEADS * 2 * DIFF_HEAD_DIM + MEM_HEADS * MEM_HEAD_DIM
A_OUT_WIDTH = MLA_HEADS * MLA_V_DIM + MEM_HEADS * MEM_HEAD_DIM
B_OUT_WIDTH = DIFF_HEADS * DIFF_V_DIM + MEM_HEADS * MEM_HEAD_DIM
SHARED_KV_WIDTH = DIFF_HEADS * (2 * DIFF_HEAD_DIM + DIFF_V_DIM)

kernel_name = 'hybrid_mla_diffattn_yoco_convffn'


def rms_norm(x, gain):
    xf = x.astype(jnp.float32)
    xf = xf * lax.rsqrt(jnp.mean(xf * xf, axis=-1, keepdims=True) + NORM_EPS)
    return xf.astype(x.dtype) * gain


def rope(x, pos):
    dh = x.shape[-1]
    half = dh // 2
    inv_freq = jnp.exp(-math.log(ROPE_THETA) * jnp.arange(half, dtype=jnp.float32) * (2.0 / dh))
    ang = pos.astype(jnp.float32)[..., None] * inv_freq
    cos = jnp.cos(ang)[:, :, None, :]
    sin = jnp.sin(ang)[:, :, None, :]
    xf = x.astype(jnp.float32)
    x1, x2 = xf[..., :half], xf[..., half:]
    return jnp.concatenate([x1 * cos - x2 * sin, x2 * cos + x1 * sin], axis=-1).astype(x.dtype)


def _to_blocks(q):
    b, s = q.shape[:2]
    return jnp.moveaxis(q.reshape((b, s // Q_BLOCK, Q_BLOCK) + q.shape[2:]), 1, 0)


def _from_blocks(o):
    nb, b = o.shape[:2]
    return jnp.moveaxis(o, 0, 1).reshape((b, nb * Q_BLOCK) + o.shape[3:])


def _causal_mask(blk, s):
    q_pos = blk * Q_BLOCK + jnp.arange(Q_BLOCK)
    return jnp.arange(s)[None, :] <= q_pos[:, None]


def causal_attention(q, k, v, scale):
    s = k.shape[1]

    def one_block(args):
        blk, q_blk = args
        sc = jnp.einsum('bqhd,bkhd->bhqk', q_blk, k).astype(jnp.float32) * scale
        sc = jnp.where(_causal_mask(blk, s), sc, -jnp.inf)
        p = jax.nn.softmax(sc, axis=-1).astype(v.dtype)
        return jnp.einsum('bhqk,bkhd->bqhd', p, v)

    nb = s // Q_BLOCK
    return _from_blocks(lax.map(one_block, (jnp.arange(nb), _to_blocks(q))))


def differential_attention(q1, q2, k1, k2, v, lam, scale):
    s = k1.shape[1]

    def one_block(args):
        blk, q1_blk, q2_blk = args
        mask = _causal_mask(blk, s)
        s1 = jnp.einsum('bqhd,bkhd->bhqk', q1_blk, k1).astype(jnp.float32) * scale
        s2 = jnp.einsum('bqhd,bkhd->bhqk', q2_blk, k2).astype(jnp.float32) * scale
        p = (jax.nn.softmax(jnp.where(mask, s1, -jnp.inf), axis=-1)
             - lam * jax.nn.softmax(jnp.where(mask, s2, -jnp.inf), axis=-1))
        return jnp.einsum('bhqk,bkhd->bqhd', p.astype(v.dtype), v)

    nb = s // Q_BLOCK
    return _from_blocks(lax.map(one_block, (jnp.arange(nb), _to_blocks(q1), _to_blocks(q2))))


def memory_attention(q_mem, mem, mem_gain, w_mem_kv):
    b, s = q_mem.shape[:2]
    m = mem.shape[1]
    q = q_mem.reshape(b, s, MEM_HEADS, MEM_HEAD_DIM)
    kv = (rms_norm(mem, mem_gain) @ w_mem_kv).reshape(b, m, MEM_HEADS, 2 * MEM_HEAD_DIM)
    k, v = kv[..., :MEM_HEAD_DIM], kv[..., MEM_HEAD_DIM:]
    sc = jnp.einsum('bshd,bmhd->bhsm', q, k).astype(jnp.float32) * (MEM_HEAD_DIM ** -0.5)
    p = jax.nn.softmax(sc, axis=-1).astype(v.dtype)
    return jnp.einsum('bhsm,bmhd->bshd', p, v).reshape(b, s, MEM_HEADS * MEM_HEAD_DIM)


def mla_mixer(h, positions, mem, w_in, q_norm, w_uq, kv_norm, w_ukv, mem_norm, w_mem_kv, w_o):
    b, s = h.shape[:2]
    proj = h @ w_in
    i0 = Q_LORA_RANK
    i1 = i0 + KV_LORA_RANK
    i2 = i1 + MLA_ROPE_DIM
    q_lat, kv_lat, k_pe, q_mem = proj[..., :i0], proj[..., i0:i1], proj[..., i1:i2], proj[..., i2:]
    q = (rms_norm(q_lat, q_norm) @ w_uq).reshape(b, s, MLA_HEADS, MLA_NOPE_DIM + MLA_ROPE_DIM)
    q = jnp.concatenate([q[..., :MLA_NOPE_DIM], rope(q[..., MLA_NOPE_DIM:], positions)], axis=-1)
    kv = (rms_norm(kv_lat, kv_norm) @ w_ukv).reshape(b, s, MLA_HEADS, MLA_NOPE_DIM + MLA_V_DIM)
    k_pe = rope(k_pe[:, :, None, :], positions)
    k = jnp.concatenate([kv[..., :MLA_NOPE_DIM],
                         jnp.broadcast_to(k_pe, (b, s, MLA_HEADS, MLA_ROPE_DIM))], axis=-1)
    v = kv[..., MLA_NOPE_DIM:]
    o_main = causal_attention(q, k, v, (MLA_NOPE_DIM + MLA_ROPE_DIM) ** -0.5)
    o_main = o_main.reshape(b, s, MLA_HEADS * MLA_V_DIM)
    o_mem = memory_attention(q_mem, mem, mem_norm, w_mem_kv)
    return jnp.concatenate([o_main, o_mem], axis=-1) @ w_o


def shared_kv(x, positions, gain, w_kv):
    b, s = x.shape[:2]
    kv = (rms_norm(x, gain) @ w_kv).reshape(b, s, DIFF_HEADS, 2 * DIFF_HEAD_DIM + DIFF_V_DIM)
    k1 = rope(kv[..., :DIFF_HEAD_DIM], positions)
    k2 = rope(kv[..., DIFF_HEAD_DIM:2 * DIFF_HEAD_DIM], positions)
    v = kv[..., 2 * DIFF_HEAD_DIM:]
    return k1, k2, v


def diff_mixer(h, positions, mem, k1, k2, v, w_in, lq1, lk1, lq2, lk2, subln,
               mem_norm, w_mem_kv, w_o, lambda_init):
    b, s = h.shape[:2]
    proj = h @ w_in
    nq = DIFF_HEADS * 2 * DIFF_HEAD_DIM
    q = proj[..., :nq].reshape(b, s, DIFF_HEADS, 2 * DIFF_HEAD_DIM)
    q1 = rope(q[..., :DIFF_HEAD_DIM], positions)
    q2 = rope(q[..., DIFF_HEAD_DIM:], positions)
    lam = (jnp.exp(jnp.sum(lq1.astype(jnp.float32) * lk1.astype(jnp.float32)))
           - jnp.exp(jnp.sum(lq2.astype(jnp.float32) * lk2.astype(jnp.float32)))
           + lambda_init)
    o = differential_attention(q1, q2, k1, k2, v, lam, DIFF_HEAD_DIM ** -0.5)
    o = (rms_norm(o, subln) * (1.0 - lambda_init)).reshape(b, s, DIFF_HEADS * DIFF_V_DIM)
    o_mem = memory_attention(proj[..., nq:], mem, mem_norm, w_mem_kv)
    return jnp.concatenate([o, o_mem], axis=-1) @ w_o


def conv_ffn(h, w_in, conv_w, conv_b, w_out):
    s = h.shape[1]
    u = h @ w_in
    up = jnp.pad(u, ((0, 0), (CONV_WIDTH - 1, 0), (0, 0)))
    c = conv_b
    for j in range(CONV_WIDTH):
        c = c + conv_w[j] * up[:, j:j + s]
    g, val = c[..., :D_FF], c[..., D_FF:]
    return (jax.nn.gelu(g, approximate=True) * val) @ w_out


def setup_inputs(seed: int = 0) -> dict:
    key = jax.random.key(seed)
    keys = list(jax.random.split(key, 48))
    f32 = jnp.float32

    def nxt():
        return keys.pop()

    def dense(shape, fan_in):
        return jax.random.normal(nxt(), shape, f32) * (fan_in ** -0.5)

    def gain(shape):
        return 1.0 + 0.05 * jax.random.normal(nxt(), shape, f32)

    def small(shape, scale):
        return scale * jax.random.normal(nxt(), shape, f32)

    na, nb = N_A_LAYERS, N_B_LAYERS
    x = jax.random.normal(nxt(), (BATCH, SEQ, D_MODEL), f32)
    mem = jax.random.normal(nxt(), (BATCH, MEM_TOKENS, D_MODEL), f32)
    offsets = jax.random.randint(nxt(), (BATCH, 1), 0, 4096, dtype=jnp.int32)
    positions = (offsets + jnp.arange(SEQ, dtype=jnp.int32)[None, :]).astype(jnp.int32)
    return {
        'x': x,
        'mem': mem,
        'positions': positions,
        'a_attn_norm_pre': gain((na, D_MODEL)),
        'a_w_in': dense((na, D_MODEL, A_IN_WIDTH), D_MODEL),
        'a_q_norm': gain((na, Q_LORA_RANK)),
        'a_w_uq': dense((na, Q_LORA_RANK, MLA_HEADS * (MLA_NOPE_DIM + MLA_ROPE_DIM)), Q_LORA_RANK),
        'a_kv_norm': gain((na, KV_LORA_RANK)),
        'a_w_ukv': dense((na, KV_LORA_RANK, MLA_HEADS * (MLA_NOPE_DIM + MLA_V_DIM)), KV_LORA_RANK),
        'a_mem_norm': gain((na, D_MODEL)),
        'a_w_mem_kv': dense((na, D_MODEL, MEM_HEADS * 2 * MEM_HEAD_DIM), D_MODEL),
        'a_w_o': dense((na, A_OUT_WIDTH, D_MODEL), A_OUT_WIDTH),
        'a_attn_norm_post': gain((na, D_MODEL)),
        'shared_kv_norm': gain((D_MODEL,)),
        'shared_w_kv': dense((D_MODEL, SHARED_KV_WIDTH), D_MODEL),
        'b_attn_norm_pre': gain((nb, D_MODEL)),
        'b_w_in': dense((nb, D_MODEL, B_IN_WIDTH), D_MODEL),
        'b_lambda_q1': small((nb, DIFF_HEAD_DIM), 0.1),
        'b_lambda_k1': small((nb, DIFF_HEAD_DIM), 0.1),
        'b_lambda_q2': small((nb, DIFF_HEAD_DIM), 0.1),
        'b_lambda_k2': small((nb, DIFF_HEAD_DIM), 0.1),
        'b_subln': gain((nb, DIFF_V_DIM)),
        'b_mem_norm': gain((nb, D_MODEL)),
        'b_w_mem_kv': dense((nb, D_MODEL, MEM_HEADS * 2 * MEM_HEAD_DIM), D_MODEL),
        'b_w_o': dense((nb, B_OUT_WIDTH, D_MODEL), B_OUT_WIDTH),
        'b_attn_norm_post': gain((nb, D_MODEL)),
        'ffn_norm_pre': gain((DEPTH, D_MODEL)),
        'ffn_w_in': dense((DEPTH, D_MODEL, 2 * D_FF), D_MODEL),
        'ffn_conv_w': dense((DEPTH, CONV_WIDTH, 2 * D_FF), CONV_WIDTH),
        'ffn_conv_b': small((DEPTH, 2 * D_FF), 0.01),
        'ffn_w_out': dense((DEPTH, D_FF, D_MODEL), D_FF),
        'ffn_norm_post': gain((DEPTH, D_MODEL)),
    }


def reference(x, mem, positions,
              a_attn_norm_pre, a_w_in, a_q_norm, a_w_uq, a_kv_norm, a_w_ukv,
              a_mem_norm, a_w_mem_kv, a_w_o, a_attn_norm_post,
              shared_kv_norm, shared_w_kv,
              b_attn_norm_pre, b_w_in, b_lambda_q1, b_lambda_k1, b_lambda_q2, b_lambda_k2,
              b_subln, b_mem_norm, b_w_mem_kv, b_w_o, b_attn_norm_post,
              ffn_norm_pre, ffn_w_in, ffn_conv_w, ffn_conv_b, ffn_w_out, ffn_norm_post):
    k1 = k2 = v = None
    for layer in range(DEPTH):
        if layer < N_A_LAYERS:
            i = layer
            mix = mla_mixer(rms_norm(x, a_attn_norm_pre[i]), positions, mem,
                            a_w_in[i], a_q_norm[i], a_w_uq[i], a_kv_norm[i], a_w_ukv[i],
                            a_mem_norm[i], a_w_mem_kv[i], a_w_o[i])
            x = x + rms_norm(mix, a_attn_norm_post[i])
        else:
            j = layer - N_A_LAYERS
            if j == 0:
                k1, k2, v = shared_kv(x, positions, shared_kv_norm, shared_w_kv)
            lambda_init = 0.8 - 0.6 * math.exp(-0.3 * layer)
            mix = diff_mixer(rms_norm(x, b_attn_norm_pre[j]), positions, mem, k1, k2, v,
                             b_w_in[j], b_lambda_q1[j], b_lambda_k1[j], b_lambda_q2[j], b_lambda_k2[j],
                             b_subln[j], b_mem_norm[j], b_w_mem_kv[j], b_w_o[j], lambda_init)
            x = x + rms_norm(mix, b_attn_norm_post[j])
        f = conv_ffn(rms_norm(x, ffn_norm_pre[layer]), ffn_w_in[layer], ffn_conv_w[layer],
                     ffn_conv_b[layer], ffn_w_out[layer])
        x = x + rms_norm(f, ffn_norm_post[layer])
    return x
```

```python
import functools
import math

import jax
import jax.numpy as jnp
from jax import lax
from jax.experimental import pallas as pl
from jax.experimental.pallas import tpu as pltpu

D_MODEL = 2048
ROPE_THETA = 10000.0
NORM_EPS = 1e-6
MLA_HEADS = 12
MLA_NOPE = 128
MLA_ROPE = 64
MLA_V = 128
Q_LORA = 512
KV_LORA = 512
DIFF_HEADS = 6
DIFF_DIM = 128
DIFF_V = 2 * DIFF_DIM
MEM_HEADS = 4
MEM_DIM = 128
D_FF = 5632
CONV_WIDTH = 3
N_A_LAYERS = 1

MLA_QPAD = 256
MLA_DK = MLA_NOPE + MLA_ROPE
A_IN_PAD = Q_LORA + KV_LORA + 128 + MEM_HEADS * MEM_DIM

V7X_VMEM_BYTES = 64 * 1024 * 1024
VMEM_LIMIT = V7X_VMEM_BYTES - 8 * 1024 * 1024
LANES = 128
SUBLANES = 8

NEG_BIG = -1e30
BF16 = jnp.bfloat16
F32 = jnp.float32


def _tiles(seq):
    tm = min(512, seq)
    tq = min(512, seq)
    fc = 512
    return tm, tq, fc


def _params(n_axes):
    return pltpu.CompilerParams(dimension_semantics=("arbitrary",) * n_axes,
                                vmem_limit_bytes=VMEM_LIMIT)


def _const_spec(shape):
    nd = len(shape)
    return pl.BlockSpec(shape, lambda *_: (0,) * nd, pipeline_mode=pl.Buffered(1))


def _rms(xf, gain):
    ms = jnp.mean(xf * xf, axis=-1, keepdims=True)
    return xf * lax.rsqrt(ms + NORM_EPS) * gain


def _dot(a, b):
    return jnp.dot(a, b, preferred_element_type=F32)


def _dot_nt(a, b):
    return lax.dot_general(a, b, (((1,), (1,)), ((), ())), preferred_element_type=F32)


def _rope_tables_rows(pos_row, half, dh):
    tokens = pos_row.shape[-1]
    j = lax.broadcasted_iota(jnp.int32, (half, tokens), 0).astype(F32)
    inv_freq = jnp.exp(j * (-math.log(ROPE_THETA) * 2.0 / dh))
    ang = pos_row.astype(F32) * inv_freq
    return jnp.cos(ang), jnp.sin(ang)


def _mem_attention(qm, mkT_ref, mv_ref, out_ref):
    scale = MEM_DIM ** -0.5
    for h in range(MEM_HEADS):
        qh = (qm[:, h * MEM_DIM:(h + 1) * MEM_DIM] * scale).astype(BF16)
        s = _dot(qh, mkT_ref[0, 0, h])
        m = jnp.max(s, axis=-1, keepdims=True)
        p = jnp.exp(s - m)
        l = jnp.sum(p, axis=-1, keepdims=True)
        o = _dot(p.astype(BF16), mv_ref[0, 0, h])
        out_ref[0, :, h * MEM_DIM:(h + 1) * MEM_DIM] = (o / l).astype(out_ref.dtype)


def _mem_kv_kernel(mem_ref, g_ref, w_ref, kT_ref, v_ref):
    n = _rms(mem_ref[0], g_ref[0]).astype(BF16)
    kv = _dot(n, w_ref[0])
    for h in range(MEM_HEADS):
        base = h * 2 * MEM_DIM
        kT_ref[0, 0, h] = kv[:, base:base + MEM_DIM].T.astype(BF16)
        v_ref[0, 0, h] = kv[:, base + MEM_DIM:base + 2 * MEM_DIM].astype(BF16)


def _mem_kv(mem, gains, weights):
    b, m, d = mem.shape
    nl = gains.shape[0]
    return pl.pallas_call(
        _mem_kv_kernel,
        grid=(nl, b),
        in_specs=[
            pl.BlockSpec((1, m, d), lambda l, i: (i, 0, 0)),
            pl.BlockSpec((1, 1, d), lambda l, i: (l, 0, 0)),
            pl.BlockSpec((1, d, MEM_HEADS * 2 * MEM_DIM), lambda l, i: (l, 0, 0)),
        ],
        out_specs=[
            pl.BlockSpec((1, 1, MEM_HEADS, MEM_DIM, m), lambda l, i: (l, i, 0, 0, 0)),
            pl.BlockSpec((1, 1, MEM_HEADS, m, MEM_DIM), lambda l, i: (l, i, 0, 0, 0)),
        ],
        out_shape=[
            jax.ShapeDtypeStruct((nl, b, MEM_HEADS, MEM_DIM, m), BF16),
            jax.ShapeDtypeStruct((nl, b, MEM_HEADS, m, MEM_DIM), BF16),
        ],
        compiler_params=_params(2),
        name="mem_kv",
    )(mem, gains, weights)


def _a_proj_kernel(x_ref, pos_ref, g_ref, w_in_ref, qg_ref, w_uq_ref, kvg_ref, w_kT_ref, w_v_ref,
                   mkT_ref, mv_ref, q_ref, kT_ref, v_ref, om_ref):
    tm = x_ref.shape[1]
    h = _rms(x_ref[0], g_ref[...]).astype(BF16)
    proj = _dot(h, w_in_ref[...])
    ql = _rms(proj[:, :Q_LORA], qg_ref[...]).astype(BF16)
    kvl = _rms(proj[:, Q_LORA:Q_LORA + KV_LORA], kvg_ref[...]).astype(BF16)
    pe = proj[:, Q_LORA + KV_LORA:Q_LORA + KV_LORA + LANES]
    qm = proj[:, Q_LORA + KV_LORA + LANES:]

    half = MLA_ROPE // 2
    cos_r, sin_r = _rope_tables_rows(pos_ref[0], half, MLA_ROPE)

    peT = pe.T
    a, b = peT[0:half], peT[half:2 * half]
    kpeT = jnp.concatenate([a * cos_r - b * sin_r, b * cos_r + a * sin_r], axis=0).astype(BF16)

    scale = MLA_DK ** -0.5
    z32 = jnp.zeros((half, tm), F32)
    z64 = jnp.zeros((2 * half, tm), F32)
    tab_c = (jnp.concatenate([cos_r, cos_r, z64], axis=0) * scale).T
    tab_s1 = (jnp.concatenate([-sin_r, z32, z64], axis=0) * scale).T
    tab_s2 = (jnp.concatenate([z32, sin_r, z64], axis=0) * scale).T

    qf = _dot(ql, w_uq_ref[...])
    kTn = _dot_nt(w_kT_ref[...], kvl)
    vf = _dot(kvl, w_v_ref[...])
    for hd in range(MLA_HEADS):
        base = hd * MLA_QPAD
        q_ref[0, hd, :, 0:MLA_NOPE] = (qf[:, base:base + MLA_NOPE] * scale).astype(BF16)
        xr = qf[:, base + MLA_NOPE:base + MLA_QPAD]
        rot = (xr * tab_c + pltpu.roll(xr, LANES - half, 1) * tab_s1
               + pltpu.roll(xr, half, 1) * tab_s2)
        q_ref[0, hd, :, MLA_NOPE:MLA_QPAD] = rot.astype(BF16)
        kT_ref[0, hd, 0, 0:MLA_NOPE, :] = kTn[hd * MLA_NOPE:(hd + 1) * MLA_NOPE].astype(BF16)
        kT_ref[0, hd, 0, MLA_NOPE:MLA_DK, :] = kpeT
        v_ref[0, hd] = vf[:, hd * MLA_V:(hd + 1) * MLA_V].astype(BF16)

    _mem_attention(qm, mkT_ref, mv_ref, om_ref)


def _a_proj(x, pos_row, gain, w_in, qg, w_uq, kvg, w_kT, w_v, mkT, mv, tm):
    b, s, d = x.shape
    nt = s // tm
    m = mkT.shape[-1]
    return pl.pallas_call(
        _a_proj_kernel,
        grid=(b, nt),
        in_specs=[
            pl.BlockSpec((1, tm, d), lambda i, t: (i, t, 0)),
            pl.BlockSpec((1, 1, tm), lambda i, t: (i, 0, t)),
            _const_spec(gain.shape),
            _const_spec(w_in.shape),
            _const_spec(qg.shape),
            _const_spec(w_uq.shape),
            _const_spec(kvg.shape),
            _const_spec(w_kT.shape),
            _const_spec(w_v.shape),
            pl.BlockSpec((1, 1, MEM_HEADS, MEM_DIM, m), lambda i, t: (0, i, 0, 0, 0)),
            pl.BlockSpec((1, 1, MEM_HEADS, m, MEM_DIM), lambda i, t: (0, i, 0, 0, 0)),
        ],
        out_specs=[
            pl.BlockSpec((1, MLA_HEADS, tm, MLA_QPAD), lambda i, t: (i, 0, t, 0)),
            pl.BlockSpec((1, MLA_HEADS, 1, MLA_DK, tm), lambda i, t: (i, 0, t, 0, 0)),
            pl.BlockSpec((1, MLA_HEADS, tm, MLA_V), lambda i, t: (i, 0, t, 0)),
            pl.BlockSpec((1, tm, MEM_HEADS * MEM_DIM), lambda i, t: (i, t, 0)),
        ],
        out_shape=[
            jax.ShapeDtypeStruct((b, MLA_HEADS, s, MLA_QPAD), BF16),
            jax.ShapeDtypeStruct((b, MLA_HEADS, nt, MLA_DK, tm), BF16),
            jax.ShapeDtypeStruct((b, MLA_HEADS, s, MLA_V), BF16),
            jax.ShapeDtypeStruct((b, s, MEM_HEADS * MEM_DIM), BF16),
        ],
        compiler_params=_params(2),
        name="a_proj",
    )(x, pos_row, gain, w_in, qg, w_uq, kvg, w_kT, w_v, mkT, mv)


def _softmax_block(s, m_prev, l_prev):
    m_new = jnp.maximum(m_prev, jnp.max(s, axis=-1, keepdims=True))
    alpha = jnp.exp(m_prev - m_new)
    p = jnp.exp(s - m_new)
    l_new = alpha * l_prev + jnp.sum(p, axis=-1, keepdims=True)
    return p, m_new, l_new, alpha


def _causal_mask(tq):
    row = lax.broadcasted_iota(jnp.int32, (tq, tq), 0)
    col = lax.broadcasted_iota(jnp.int32, (tq, tq), 1)
    return col <= row


def _mla_attn_kernel(q_ref, kT_ref, v_ref, o_ref, acc_ref):
    tq = q_ref.shape[2]
    qi = pl.program_id(2)
    q = q_ref[0, 0][:, :MLA_DK]

    s = _dot(q, kT_ref[0, 0, qi])
    s = jnp.where(_causal_mask(tq), s, NEG_BIG)
    m0 = jnp.max(s, axis=-1, keepdims=True)
    p = jnp.exp(s - m0)
    l0 = jnp.sum(p, axis=-1, keepdims=True)
    start = pl.multiple_of(qi * tq, tq)
    acc_ref[...] = _dot(p.astype(BF16), v_ref[0, 0, pl.ds(start, tq), :])

    def body(kb, carry):
        m_prev, l_prev = carry
        s = _dot(q, kT_ref[0, 0, kb])
        p, m_new, l_new, alpha = _softmax_block(s, m_prev, l_prev)
        off = pl.multiple_of(kb * tq, tq)
        acc_ref[...] = alpha * acc_ref[...] + _dot(p.astype(BF16), v_ref[0, 0, pl.ds(off, tq), :])
        return m_new, l_new

    _, l_fin = lax.fori_loop(0, qi, body, (m0, l0))
    o_ref[0] = (acc_ref[...] / l_fin).astype(o_ref.dtype)


def _mla_attn(q, kT, v, tq):
    b, nh, s, _ = q.shape
    nq = s // tq
    return pl.pallas_call(
        _mla_attn_kernel,
        grid=(b, nh, nq),
        in_specs=[
            pl.BlockSpec((1, 1, tq, MLA_QPAD), lambda i, h, t: (i, h, t, 0)),
            pl.BlockSpec((1, 1, nq, MLA_DK, tq), lambda i, h, t: (i, h, 0, 0, 0)),
            pl.BlockSpec((1, 1, s, MLA_V), lambda i, h, t: (i, h, 0, 0)),
        ],
        out_specs=pl.BlockSpec((1, tq, MLA_V), lambda i, h, t: (i, t, h)),
        out_shape=jax.ShapeDtypeStruct((b, s, nh * MLA_V), BF16),
        scratch_shapes=[pltpu.VMEM((tq, MLA_V), F32)],
        compiler_params=_params(3),
        name="mla_attn",
    )(q, kT, v)


def _out_proj_kernel(o_ref, om_ref, x_ref, w1_ref, w2_ref, g_ref, out_ref):
    mix = _dot(o_ref[0], w1_ref[...]) + _dot(om_ref[0], w2_ref[...])
    out_ref[0] = x_ref[0] + _rms(mix, g_ref[...])


def _out_proj(o, om, x, w1, w2, gain, tm):
    b, s, d = x.shape
    return pl.pallas_call(
        _out_proj_kernel,
        grid=(b, s // tm),
        in_specs=[
            pl.BlockSpec((1, tm, o.shape[-1]), lambda i, t: (i, t, 0)),
            pl.BlockSpec((1, tm, om.shape[-1]), lambda i, t: (i, t, 0)),
            pl.BlockSpec((1, tm, d), lambda i, t: (i, t, 0)),
            _const_spec(w1.shape),
            _const_spec(w2.shape),
            _const_spec(gain.shape),
        ],
        out_specs=pl.BlockSpec((1, tm, d), lambda i, t: (i, t, 0)),
        out_shape=jax.ShapeDtypeStruct((b, s, d), F32),
        compiler_params=_params(2),
        name="out_proj",
    )(o, om, x, w1, w2, gain)


def _gelu_tanh(g):
    c = math.sqrt(2.0 / math.pi)
    return 0.5 * g * (1.0 + jnp.tanh(c * (g + 0.044715 * (g * g * g))))


def _causal_conv(u, tail, cw, cb):
    tm = u.shape[0]
    ext = jnp.concatenate([tail, u], axis=0)
    u1 = pltpu.roll(ext, 1, 0)[SUBLANES:]
    u2 = pltpu.roll(ext, 2, 0)[SUBLANES:]
    del tm
    return cb + cw[0:1] * u2 + cw[1:2] * u1 + cw[2:3] * u


def _conv_ffn_kernel(x_ref, pre_ref, wg_ref, wv_ref, cwg_ref, cwv_ref, cbg_ref, cbv_ref, wo_ref,
                     post_ref, out_ref, h_ref, acc_ref, tail_ref):
    t = pl.program_id(1)
    c = pl.program_id(2)
    nc = pl.num_programs(2)
    tm = x_ref.shape[1]

    @pl.when(c == 0)
    def _():
        h_ref[...] = _rms(x_ref[0], pre_ref[...]).astype(BF16)

    h = h_ref[...]
    ug = _dot(h, wg_ref[...])
    uv = _dot(h, wv_ref[...])
    @pl.when(t == 0)
    def _():
        tail_ref[c] = jnp.zeros(tail_ref.shape[1:], F32)

    tail_g = tail_ref[c, 0]
    tail_v = tail_ref[c, 1]
    tail_ref[c, 0] = ug[tm - SUBLANES:]
    tail_ref[c, 1] = uv[tm - SUBLANES:]
    cg = _causal_conv(ug, tail_g, cwg_ref[...], cbg_ref[...])
    cv = _causal_conv(uv, tail_v, cwv_ref[...], cbv_ref[...])
    y = (_gelu_tanh(cg) * cv).astype(BF16)
    part = _dot(y, wo_ref[...])

    @pl.when(c == 0)
    def _():
        acc_ref[...] = part

    @pl.when(c > 0)
    def _():
        acc_ref[...] += part

    @pl.when(c == nc - 1)
    def _():
        out_ref[0] = x_ref[0] + _rms(acc_ref[...], post_ref[...])


def _conv_ffn(x, pre, w_in, conv_w, conv_b, w_out, post, tm, fc):
    b, s, d = x.shape
    nc = D_FF // fc
    return pl.pallas_call(
        _conv_ffn_kernel,
        grid=(b, s // tm, nc),
        in_specs=[
            pl.BlockSpec((1, tm, d), lambda i, t, c: (i, t, 0)),
            _const_spec(pre.shape),
            pl.BlockSpec((d, fc), lambda i, t, c: (0, c)),
            pl.BlockSpec((d, fc), lambda i, t, c: (0, c + nc)),
            pl.BlockSpec((CONV_WIDTH, fc), lambda i, t, c: (0, c)),
            pl.BlockSpec((CONV_WIDTH, fc), lambda i, t, c: (0, c + nc)),
            pl.BlockSpec((1, fc), lambda i, t, c: (0, c)),
            pl.BlockSpec((1, fc), lambda i, t, c: (0, c + nc)),
            pl.BlockSpec((fc, d), lambda i, t, c: (c, 0)),
            _const_spec(post.shape),
        ],
        out_specs=pl.BlockSpec((1, tm, d), lambda i, t, c: (i, t, 0)),
        out_shape=jax.ShapeDtypeStruct((b, s, d), F32),
        scratch_shapes=[
            pltpu.VMEM((tm, d), BF16),
            pltpu.VMEM((tm, d), F32),
            pltpu.VMEM((nc, 2, SUBLANES, fc), F32),
        ],
        compiler_params=_params(3),
        name="conv_ffn",
    )(x, pre, w_in, w_in, conv_w, conv_w, conv_b, conv_b, w_out, post)


def _b_proj_kernel(x_ref, pos_ref, gs_ref, gb_ref, w_kT_ref, w_v_ref, w_in_ref, mkT_ref, mv_ref,
                   q1_ref, q2_ref, k1T_ref, k2T_ref, v_ref, om_ref):
    xf = x_ref[0]
    n = xf * lax.rsqrt(jnp.mean(xf * xf, axis=-1, keepdims=True) + NORM_EPS)
    hs = (n * gs_ref[...]).astype(BF16)
    hb = (n * gb_ref[...]).astype(BF16)

    half = DIFF_DIM // 2
    cos_r, sin_r = _rope_tables_rows(pos_ref[0], half, DIFF_DIM)

    kT = _dot_nt(w_kT_ref[...], hs)
    vf = _dot(hs, w_v_ref[...])
    nk = DIFF_HEADS * DIFF_DIM
    for hd in range(DIFF_HEADS):
        for which, ref in ((0, k1T_ref), (1, k2T_ref)):
            base = which * nk + hd * DIFF_DIM
            a = kT[base:base + half]
            b = kT[base + half:base + DIFF_DIM]
            ref[0, hd, 0] = jnp.concatenate([a * cos_r - b * sin_r, b * cos_r + a * sin_r],
                                            axis=0).astype(BF16)
        v_ref[0, hd] = vf[:, hd * DIFF_V:(hd + 1) * DIFF_V].astype(BF16)

    scale = DIFF_DIM ** -0.5
    tab_c = (jnp.concatenate([cos_r, cos_r], axis=0) * scale).T
    tab_s = (jnp.concatenate([-sin_r, sin_r], axis=0) * scale).T
    proj = _dot(hb, w_in_ref[...])
    for hd in range(DIFF_HEADS):
        for which, ref in ((0, q1_ref), (1, q2_ref)):
            base = hd * 2 * DIFF_DIM + which * DIFF_DIM
            xr = proj[:, base:base + DIFF_DIM]
            ref[0, hd] = (xr * tab_c + pltpu.roll(xr, half, 1) * tab_s).astype(BF16)

    _mem_attention(proj[:, DIFF_HEADS * 2 * DIFF_DIM:], mkT_ref, mv_ref, om_ref)


def _b_proj(x, pos_row, gs, gb, w_kT, w_v, w_in, mkT, mv, tm):
    b, s, d = x.shape
    nt = s // tm
    m = mkT.shape[-1]
    qk_shape = jax.ShapeDtypeStruct((b, DIFF_HEADS, s, DIFF_DIM), BF16)
    kT_shape = jax.ShapeDtypeStruct((b, DIFF_HEADS, nt, DIFF_DIM, tm), BF16)
    q_spec = pl.BlockSpec((1, DIFF_HEADS, tm, DIFF_DIM), lambda i, t: (i, 0, t, 0))
    kT_spec = pl.BlockSpec((1, DIFF_HEADS, 1, DIFF_DIM, tm), lambda i, t: (i, 0, t, 0, 0))
    return pl.pallas_call(
        _b_proj_kernel,
        grid=(b, nt),
        in_specs=[
            pl.BlockSpec((1, tm, d), lambda i, t: (i, t, 0)),
            pl.BlockSpec((1, 1, tm), lambda i, t: (i, 0, t)),
            _const_spec(gs.shape),
            _const_spec(gb.shape),
            _const_spec(w_kT.shape),
            _const_spec(w_v.shape),
            _const_spec(w_in.shape),
            pl.BlockSpec((1, 1, MEM_HEADS, MEM_DIM, m), lambda i, t: (1, i, 0, 0, 0)),
            pl.BlockSpec((1, 1, MEM_HEADS, m, MEM_DIM), lambda i, t: (1, i, 0, 0, 0)),
        ],
        out_specs=[
            q_spec, q_spec, kT_spec, kT_spec,
            pl.BlockSpec((1, DIFF_HEADS, tm, DIFF_V), lambda i, t: (i, 0, t, 0)),
            pl.BlockSpec((1, tm, MEM_HEADS * MEM_DIM), lambda i, t: (i, t, 0)),
        ],
        out_shape=[
            qk_shape, qk_shape, kT_shape, kT_shape,
            jax.ShapeDtypeStruct((b, DIFF_HEADS, s, DIFF_V), BF16),
            jax.ShapeDtypeStruct((b, s, MEM_HEADS * MEM_DIM), BF16),
        ],
        compiler_params=_params(2),
        name="b_proj",
    )(x, pos_row, gs, gb, w_kT, w_v, w_in, mkT, mv)


def _diff_attn_kernel(lam_init, q1_ref, q2_ref, k1T_ref, k2T_ref, v_ref, lam_ref, sub_ref, o_ref,
                      acc1_ref, acc2_ref):
    tq = q1_ref.shape[2]
    qi = pl.program_id(2)
    q1 = q1_ref[0, 0]
    q2 = q2_ref[0, 0]
    mask = _causal_mask(tq)
    start = pl.multiple_of(qi * tq, tq)
    v_diag = v_ref[0, 0, pl.ds(start, tq), :]

    def diag(q, kT_ref, acc_ref):
        s = jnp.where(mask, _dot(q, kT_ref[0, 0, qi]), NEG_BIG)
        m0 = jnp.max(s, axis=-1, keepdims=True)
        p = jnp.exp(s - m0)
        acc_ref[...] = _dot(p.astype(BF16), v_diag)
        return m0, jnp.sum(p, axis=-1, keepdims=True)

    ma, la = diag(q1, k1T_ref, acc1_ref)
    mb, lb = diag(q2, k2T_ref, acc2_ref)

    def body(kb, carry):
        ma, la, mb, lb = carry
        off = pl.multiple_of(kb * tq, tq)
        vb = v_ref[0, 0, pl.ds(off, tq), :]
        pa, ma, la, aa = _softmax_block(_dot(q1, k1T_ref[0, 0, kb]), ma, la)
        acc1_ref[...] = aa * acc1_ref[...] + _dot(pa.astype(BF16), vb)
        pb, mb, lb, ab = _softmax_block(_dot(q2, k2T_ref[0, 0, kb]), mb, lb)
        acc2_ref[...] = ab * acc2_ref[...] + _dot(pb.astype(BF16), vb)
        return ma, la, mb, lb

    _, la, _, lb = lax.fori_loop(0, qi, body, (ma, la, mb, lb))

    lp = lam_ref[...]
    lam = (jnp.exp(jnp.sum(lp[0:1] * lp[1:2], axis=-1, keepdims=True))
           - jnp.exp(jnp.sum(lp[2:3] * lp[3:4], axis=-1, keepdims=True)) + lam_init)
    o = acc1_ref[...] / la - lam * (acc2_ref[...] / lb)
    o_ref[0] = (_rms(o, sub_ref[...]) * (1.0 - lam_init)).astype(o_ref.dtype)


def _diff_attn(q1, q2, k1T, k2T, v, lam_params, subln, lam_init, tq):
    b, nh, s, _ = q1.shape
    nq = s // tq
    q_spec = pl.BlockSpec((1, 1, tq, DIFF_DIM), lambda i, h, t: (i, h, t, 0))
    kT_spec = pl.BlockSpec((1, 1, nq, DIFF_DIM, tq), lambda i, h, t: (i, h, 0, 0, 0))
    return pl.pallas_call(
        functools.partial(_diff_attn_kernel, lam_init),
        grid=(b, nh, nq),
        in_specs=[
            q_spec, q_spec, kT_spec, kT_spec,
            pl.BlockSpec((1, 1, s, DIFF_V), lambda i, h, t: (i, h, 0, 0)),
            _const_spec(lam_params.shape),
            _const_spec(subln.shape),
        ],
        out_specs=pl.BlockSpec((1, tq, DIFF_V), lambda i, h, t: (i, t, h)),
        out_shape=jax.ShapeDtypeStruct((b, s, nh * DIFF_V), BF16),
        scratch_shapes=[pltpu.VMEM((tq, DIFF_V), F32), pltpu.VMEM((tq, DIFF_V), F32)],
        compiler_params=_params(3),
        name="diff_attn",
    )(q1, q2, k1T, k2T, v, lam_params, subln)


def _row(v):
    return v.reshape(1, -1).astype(F32)


def kernel(x, mem, positions, a_attn_norm_pre, a_w_in, a_q_norm, a_w_uq, a_kv_norm, a_w_ukv, a_mem_norm, a_w_mem_kv, a_w_o, a_attn_norm_post, shared_kv_norm, shared_w_kv, b_attn_norm_pre, b_w_in, b_lambda_q1, b_lambda_k1, b_lambda_q2, b_lambda_k2, b_subln, b_mem_norm, b_w_mem_kv, b_w_o, b_attn_norm_post, ffn_norm_pre, ffn_w_in, ffn_conv_w, ffn_conv_b, ffn_w_out, ffn_norm_post):
    b, s, d = x.shape
    assert d == D_MODEL and a_attn_norm_pre.shape[0] == 1 and b_attn_norm_pre.shape[0] == 1
    tm, tq, fc = _tiles(s)
    assert s % tm == 0 and s % tq == 0 and tm == tq and D_FF % fc == 0
    pos_row = positions.reshape(b, 1, s)

    mem_gains = jnp.stack([a_mem_norm[0], b_mem_norm[0]]).reshape(2, 1, d)
    mem_w = jnp.stack([a_w_mem_kv[0], b_w_mem_kv[0]]).astype(BF16)
    mkT, mv = _mem_kv(mem, mem_gains, mem_w)

    w_in = a_w_in[0]
    i1 = Q_LORA + KV_LORA
    i2 = i1 + MLA_ROPE
    w_in_p = jnp.concatenate(
        [w_in[:, :i2], jnp.zeros((d, LANES - MLA_ROPE), F32), w_in[:, i2:]], axis=1).astype(BF16)
    w_uq = a_w_uq[0].reshape(Q_LORA, MLA_HEADS, MLA_DK)
    w_uq_p = jnp.concatenate(
        [w_uq, jnp.zeros((Q_LORA, MLA_HEADS, MLA_QPAD - MLA_DK), F32)], axis=2
    ).reshape(Q_LORA, MLA_HEADS * MLA_QPAD).astype(BF16)
    w_ukv = a_w_ukv[0].reshape(KV_LORA, MLA_HEADS, MLA_NOPE + MLA_V)
    w_kT = w_ukv[:, :, :MLA_NOPE].reshape(KV_LORA, MLA_HEADS * MLA_NOPE).T.astype(BF16)
    w_v = w_ukv[:, :, MLA_NOPE:].reshape(KV_LORA, MLA_HEADS * MLA_V).astype(BF16)
    q, kT, v, om = _a_proj(x, pos_row, _row(a_attn_norm_pre[0]), w_in_p, _row(a_q_norm[0]), w_uq_p,
                           _row(a_kv_norm[0]), w_kT, w_v, mkT, mv, tm)
    o = _mla_attn(q, kT, v, tq)
    n_main = MLA_HEADS * MLA_V
    w_o = a_w_o[0].astype(BF16)
    x = _out_proj(o, om, x, w_o[:n_main], w_o[n_main:], _row(a_attn_norm_post[0]), tm)
    x = _conv_ffn(x, _row(ffn_norm_pre[0]), ffn_w_in[0].astype(BF16), ffn_conv_w[0],
                  _row(ffn_conv_b[0]), ffn_w_out[0].astype(BF16), _row(ffn_norm_post[0]), tm, fc)

    layer = N_A_LAYERS
    lam_init = 0.8 - 0.6 * math.exp(-0.3 * layer)
    w_kv = shared_w_kv.reshape(d, DIFF_HEADS, 2 * DIFF_DIM + DIFF_V)
    nk = DIFF_HEADS * DIFF_DIM
    w_k12T = jnp.concatenate(
        [w_kv[:, :, :DIFF_DIM].reshape(d, nk), w_kv[:, :, DIFF_DIM:2 * DIFF_DIM].reshape(d, nk)],
        axis=1).T.astype(BF16)
    w_sv = w_kv[:, :, 2 * DIFF_DIM:].reshape(d, DIFF_HEADS * DIFF_V).astype(BF16)
    q1, q2, k1T, k2T, vb, omb = _b_proj(x, pos_row, _row(shared_kv_norm), _row(b_attn_norm_pre[0]),
                                        w_k12T, w_sv, b_w_in[0].astype(BF16), mkT, mv, tm)
    lam_params = jnp.stack([b_lambda_q1[0], b_lambda_k1[0], b_lambda_q2[0], b_lambda_k2[0]]).astype(F32)
    ob = _diff_attn(q1, q2, k1T, k2T, vb, lam_params, _row(b_subln[0]), lam_init, tq)
    n_main_b = DIFF_HEADS * DIFF_V
    w_ob = b_w_o[0].astype(BF16)
    x = _out_proj(ob, omb, x, w_ob[:n_main_b], w_ob[n_main_b:], _row(b_attn_norm_post[0]), tm)
    x = _conv_ffn(x, _row(ffn_norm_pre[1]), ffn_w_in[1].astype(BF16), ffn_conv_w[1],
                  _row(ffn_conv_b[1]), ffn_w_out[1].astype(BF16), _row(ffn_norm_post[1]), tm, fc)
    return x
```

```python
import functools
import math

import jax
import jax.numpy as jnp
from jax import lax
from jax.experimental import pallas as pl
from jax.experimental.pallas import tpu as pltpu

D_MODEL = 2048
ROPE_THETA = 10000.0
NORM_EPS = 1e-6
MLA_HEADS = 12
MLA_NOPE = 128
MLA_ROPE = 64
MLA_V = 128
MLA_DK = MLA_NOPE + MLA_ROPE
Q_LORA = 512
KV_LORA = 512
DIFF_HEADS = 6
DIFF_DIM = 128
DIFF_V = 2 * DIFF_DIM
MEM_HEADS = 4
MEM_DIM = 128
D_FF = 5632
CONV_WIDTH = 3
N_A_LAYERS = 1

LANES = 128
SUBLANES = 8
A_IN_PAD = Q_LORA + KV_LORA + LANES + MEM_HEADS * MEM_DIM

V7X_VMEM_BYTES = 64 * 1024 * 1024
VMEM_LIMIT = V7X_VMEM_BYTES - 8 * 1024 * 1024

NEG_BIG = -1e30
LOG2_E = math.log2(math.e)
BF16 = jnp.bfloat16
F32 = jnp.float32


def _tiles(seq):
    tm = min(512, seq)
    tq = min(512, seq)
    fc = 512
    return tm, tq, fc


def _params(n_axes):
    return pltpu.CompilerParams(dimension_semantics=("arbitrary",) * n_axes,
                                vmem_limit_bytes=VMEM_LIMIT)


def _const_spec(shape):
    nd = len(shape)
    return pl.BlockSpec(shape, lambda *_: (0,) * nd, pipeline_mode=pl.Buffered(1))


def _rms(xf, gain):
    ms = jnp.mean(xf * xf, axis=-1, keepdims=True)
    return xf * lax.rsqrt(ms + NORM_EPS) * gain


def _dot(a, b):
    return jnp.dot(a, b, preferred_element_type=F32)


def _dot_nt(a, b):
    return lax.dot_general(a, b, (((1,), (1,)), ((), ())), preferred_element_type=F32)


def _rope_tables_rows(pos_row, half, dh):
    tokens = pos_row.shape[-1]
    j = lax.broadcasted_iota(jnp.int32, (half, tokens), 0).astype(F32)
    inv_freq = jnp.exp(j * (-math.log(ROPE_THETA) * 2.0 / dh))
    ang = pos_row.astype(F32) * inv_freq
    return jnp.cos(ang), jnp.sin(ang)


def _rope_rows(a, b, cos_r, sin_r):
    return a * cos_r - b * sin_r, b * cos_r + a * sin_r


def _mem_attention(qm, mkT_ref, mv_ref, out_ref):
    scale = MEM_DIM ** -0.5
    for h in range(MEM_HEADS):
        qh = (qm[:, h * MEM_DIM:(h + 1) * MEM_DIM] * scale).astype(BF16)
        s = _dot(qh, mkT_ref[0, 0, h])
        m = jnp.max(s, axis=-1, keepdims=True)
        p = jnp.exp(s - m)
        l = jnp.sum(p, axis=-1, keepdims=True)
        o = _dot(p.astype(BF16), mv_ref[0, 0, h])
        out_ref[0, :, h * MEM_DIM:(h + 1) * MEM_DIM] = (o / l).astype(out_ref.dtype)


def _mem_kv_kernel(mem_ref, g_ref, w_ref, kT_ref, v_ref):
    n = _rms(mem_ref[0], g_ref[0]).astype(BF16)
    kv = _dot(n, w_ref[0])
    for h in range(MEM_HEADS):
        base = h * 2 * MEM_DIM
        kT_ref[0, 0, h] = kv[:, base:base + MEM_DIM].T.astype(BF16)
        v_ref[0, 0, h] = kv[:, base + MEM_DIM:base + 2 * MEM_DIM].astype(BF16)


def _mem_kv(mem, gains, weights):
    b, m, d = mem.shape
    nl = gains.shape[0]
    return pl.pallas_call(
        _mem_kv_kernel,
        grid=(nl, b),
        in_specs=[
            pl.BlockSpec((1, m, d), lambda l, i: (i, 0, 0)),
            pl.BlockSpec((1, 1, d), lambda l, i: (l, 0, 0)),
            pl.BlockSpec((1, d, MEM_HEADS * 2 * MEM_DIM), lambda l, i: (l, 0, 0)),
        ],
        out_specs=[
            pl.BlockSpec((1, 1, MEM_HEADS, MEM_DIM, m), lambda l, i: (l, i, 0, 0, 0)),
            pl.BlockSpec((1, 1, MEM_HEADS, m, MEM_DIM), lambda l, i: (l, i, 0, 0, 0)),
        ],
        out_shape=[
            jax.ShapeDtypeStruct((nl, b, MEM_HEADS, MEM_DIM, m), BF16),
            jax.ShapeDtypeStruct((nl, b, MEM_HEADS, m, MEM_DIM), BF16),
        ],
        compiler_params=_params(2),
        name="mem_kv",
    )(mem, gains, weights)


def _a_proj_kernel(x_ref, pos_ref, g_ref, w_in_ref, qg_ref, w_uqT_ref, kvg_ref, w_k_ref, w_vT_ref,
                   mkT_ref, mv_ref, qT_ref, k_ref, vT_ref, om_ref):
    tm = x_ref.shape[1]
    h = _rms(x_ref[0], g_ref[...]).astype(BF16)
    proj = _dot(h, w_in_ref[...])
    ql = _rms(proj[:, :Q_LORA], qg_ref[...]).astype(BF16)
    kvl = _rms(proj[:, Q_LORA:Q_LORA + KV_LORA], kvg_ref[...]).astype(BF16)
    pe = proj[:, Q_LORA + KV_LORA:Q_LORA + KV_LORA + LANES]
    qm = proj[:, Q_LORA + KV_LORA + LANES:]

    half = MLA_ROPE // 2
    cos_r, sin_r = _rope_tables_rows(pos_ref[0], half, MLA_ROPE)

    peT = pe.T
    r1, r2 = _rope_rows(peT[0:half], peT[half:2 * half], cos_r, sin_r)
    kpe = jnp.concatenate([r1, r2, jnp.zeros((LANES - MLA_ROPE, tm), F32)], axis=0).T
    kpe = kpe[:, :MLA_ROPE].astype(BF16)

    scale = MLA_DK ** -0.5 * LOG2_E
    cos_q, sin_q = cos_r * scale, sin_r * scale
    qT = _dot_nt(w_uqT_ref[...], ql)
    kn = _dot(kvl, w_k_ref[...])
    vT = _dot_nt(w_vT_ref[...], kvl)
    for hd in range(MLA_HEADS):
        base = hd * MLA_DK
        qT_ref[0, hd, 0, 0:MLA_NOPE, :] = (qT[base:base + MLA_NOPE] * scale).astype(BF16)
        r1, r2 = _rope_rows(qT[base + MLA_NOPE:base + MLA_NOPE + half],
                            qT[base + MLA_NOPE + half:base + MLA_DK], cos_q, sin_q)
        qT_ref[0, hd, 0, MLA_NOPE:MLA_NOPE + half, :] = r1.astype(BF16)
        qT_ref[0, hd, 0, MLA_NOPE + half:MLA_DK, :] = r2.astype(BF16)
        k_ref[0, hd, :, 0:MLA_NOPE] = kn[:, hd * MLA_NOPE:(hd + 1) * MLA_NOPE].astype(BF16)
        k_ref[0, hd, :, MLA_NOPE:MLA_DK] = kpe
        vT_ref[0, hd, 0] = vT[hd * MLA_V:(hd + 1) * MLA_V].astype(BF16)

    _mem_attention(qm, mkT_ref, mv_ref, om_ref)


def _a_proj(x, pos_row, gain, w_in, qg, w_uqT, kvg, w_k, w_vT, mkT, mv, tm):
    b, s, d = x.shape
    nt = s // tm
    m = mkT.shape[-1]
    return pl.pallas_call(
        _a_proj_kernel,
        grid=(b, nt),
        in_specs=[
            pl.BlockSpec((1, tm, d), lambda i, t: (i, t, 0)),
            pl.BlockSpec((1, 1, tm), lambda i, t: (i, 0, t)),
            _const_spec(gain.shape),
            _const_spec(w_in.shape),
            _const_spec(qg.shape),
            _const_spec(w_uqT.shape),
            _const_spec(kvg.shape),
            _const_spec(w_k.shape),
            _const_spec(w_vT.shape),
            pl.BlockSpec((1, 1, MEM_HEADS, MEM_DIM, m), lambda i, t: (0, i, 0, 0, 0)),
            pl.BlockSpec((1, 1, MEM_HEADS, m, MEM_DIM), lambda i, t: (0, i, 0, 0, 0)),
        ],
        out_specs=[
            pl.BlockSpec((1, MLA_HEADS, 1, MLA_DK, tm), lambda i, t: (i, 0, t, 0, 0)),
            pl.BlockSpec((1, MLA_HEADS, tm, MLA_DK), lambda i, t: (i, 0, t, 0)),
            pl.BlockSpec((1, MLA_HEADS, 1, MLA_V, tm), lambda i, t: (i, 0, t, 0, 0)),
            pl.BlockSpec((1, tm, MEM_HEADS * MEM_DIM), lambda i, t: (i, t, 0)),
        ],
        out_shape=[
            jax.ShapeDtypeStruct((b, MLA_HEADS, nt, MLA_DK, tm), BF16),
            jax.ShapeDtypeStruct((b, MLA_HEADS, s, MLA_DK), BF16),
            jax.ShapeDtypeStruct((b, MLA_HEADS, nt, MLA_V, tm), BF16),
            jax.ShapeDtypeStruct((b, s, MEM_HEADS * MEM_DIM), BF16),
        ],
        compiler_params=_params(2),
        name="a_proj",
    )(x, pos_row, gain, w_in, qg, w_uqT, kvg, w_k, w_vT, mkT, mv)


def _causal_mask_t(tq):
    key = lax.broadcasted_iota(jnp.int32, (tq, tq), 0)
    qry = lax.broadcasted_iota(jnp.int32, (tq, tq), 1)
    return key <= qry


def _online_softmax_t(s, m_prev, l_prev, mask):
    if mask is not None:
        s = jnp.where(mask, s, NEG_BIG)
    m_new = jnp.maximum(m_prev, jnp.max(s, axis=0, keepdims=True))
    alpha = jnp.exp2(m_prev - m_new)
    p = jnp.exp2(s - m_new)
    l_new = alpha * l_prev + jnp.sum(p, axis=0, keepdims=True)
    return p.astype(BF16), m_new, l_new, alpha


KEY_BLOCK_UNROLL = 4


def _pipelined_key_blocks(qi, scores, consume, finish, init, mask):
    scores(0, 0)

    def group(g, carry):
        j = KEY_BLOCK_UNROLL * g
        for u in range(KEY_BLOCK_UNROLL):
            scores(j + u + 1, (u + 1) % 2)
            carry = consume(u % 2, j + u, carry, None)
        return carry

    carry = lax.fori_loop(0, qi // KEY_BLOCK_UNROLL, group, init)
    base = (qi // KEY_BLOCK_UNROLL) * KEY_BLOCK_UNROLL

    for rem in range(KEY_BLOCK_UNROLL):
        @pl.when(qi % KEY_BLOCK_UNROLL == rem)
        def _(rem=rem):
            c = carry
            for u in range(rem):
                scores(base + u + 1, (u + 1) % 2)
                c = consume(u % 2, base + u, c, None)
            finish(consume(rem % 2, qi, c, mask))


def _mla_attn_kernel(qT_ref, k_ref, vT_ref, o_ref, sa_ref, sb_ref, acc_ref):
    tq = qT_ref.shape[4]
    qi = pl.program_id(2)
    qT = qT_ref[0, 0, 0]
    s_refs = (sa_ref, sb_ref)

    def scores(j, slot):
        off = pl.multiple_of(j * tq, tq)
        s_refs[slot][...] = _dot(k_ref[0, 0, pl.ds(off, tq), :], qT)

    def consume(slot, j, carry, mask):
        m_prev, l_prev = carry
        p, m_new, l_new, alpha = _online_softmax_t(s_refs[slot][...], m_prev, l_prev, mask)
        acc_ref[...] = alpha * acc_ref[...] + _dot(vT_ref[0, 0, j], p)
        return m_new, l_new

    def finish(carry):
        _, l = carry
        o_ref[0] = (acc_ref[...] / l).T.astype(o_ref.dtype)

    acc_ref[...] = jnp.zeros(acc_ref.shape, F32)
    init = (jnp.full((1, tq), NEG_BIG, F32), jnp.zeros((1, tq), F32))
    _pipelined_key_blocks(qi, scores, consume, finish, init, _causal_mask_t(tq))


def _mla_attn(qT, k, vT, tq):
    b, nh, nq, _, _ = qT.shape
    s = nq * tq
    return pl.pallas_call(
        _mla_attn_kernel,
        grid=(b, nh, nq),
        in_specs=[
            pl.BlockSpec((1, 1, 1, MLA_DK, tq), lambda i, h, t: (i, h, t, 0, 0)),
            pl.BlockSpec((1, 1, s, MLA_DK), lambda i, h, t: (i, h, 0, 0)),
            pl.BlockSpec((1, 1, nq, MLA_V, tq), lambda i, h, t: (i, h, 0, 0, 0)),
        ],
        out_specs=pl.BlockSpec((1, tq, MLA_V), lambda i, h, t: (i, t, h)),
        out_shape=jax.ShapeDtypeStruct((b, s, nh * MLA_V), BF16),
        scratch_shapes=[pltpu.VMEM((tq, tq), F32), pltpu.VMEM((tq, tq), F32),
                        pltpu.VMEM((MLA_V, tq), F32)],
        compiler_params=_params(3),
        name="mla_attn",
    )(qT, k, vT)


def _diff_attn_kernel(lam_init, q1T_ref, q2T_ref, k1_ref, k2_ref, vT_ref, lam_ref, sub_ref, o_ref,
                      s1a_ref, s1b_ref, s2a_ref, s2b_ref, acc1_ref, acc2_ref):
    tq = q1T_ref.shape[4]
    qi = pl.program_id(2)
    q1T = q1T_ref[0, 0, 0]
    q2T = q2T_ref[0, 0, 0]
    s1_refs = (s1a_ref, s1b_ref)
    s2_refs = (s2a_ref, s2b_ref)

    def scores(j, slot):
        off = pl.multiple_of(j * tq, tq)
        s1_refs[slot][...] = _dot(k1_ref[0, 0, pl.ds(off, tq), :], q1T)
        s2_refs[slot][...] = _dot(k2_ref[0, 0, pl.ds(off, tq), :], q2T)

    def consume(slot, j, carry, mask):
        m1, l1, m2, l2 = carry
        vT = vT_ref[0, 0, j]
        p1, m1, l1, a1 = _online_softmax_t(s1_refs[slot][...], m1, l1, mask)
        acc1_ref[...] = a1 * acc1_ref[...] + _dot(vT, p1)
        p2, m2, l2, a2 = _online_softmax_t(s2_refs[slot][...], m2, l2, mask)
        acc2_ref[...] = a2 * acc2_ref[...] + _dot(vT, p2)
        return m1, l1, m2, l2

    def finish(carry):
        _, l1, _, l2 = carry
        lp = lam_ref[...]
        lam = (jnp.exp(jnp.sum(lp[0:1] * lp[1:2], axis=-1, keepdims=True))
               - jnp.exp(jnp.sum(lp[2:3] * lp[3:4], axis=-1, keepdims=True)) + lam_init)
        oT = acc1_ref[...] / l1 - lam * (acc2_ref[...] / l2)
        o_ref[0] = (_rms(oT.T, sub_ref[...]) * (1.0 - lam_init)).astype(o_ref.dtype)

    acc1_ref[...] = jnp.zeros(acc1_ref.shape, F32)
    acc2_ref[...] = jnp.zeros(acc2_ref.shape, F32)
    m_init = jnp.full((1, tq), NEG_BIG, F32)
    l_init = jnp.zeros((1, tq), F32)
    _pipelined_key_blocks(qi, scores, consume, finish, (m_init, l_init, m_init, l_init),
                          _causal_mask_t(tq))


def _diff_attn(q1T, q2T, k1, k2, vT, lam_params, subln, lam_init, tq):
    b, nh, nq, _, _ = q1T.shape
    s = nq * tq
    qT_spec = pl.BlockSpec((1, 1, 1, DIFF_DIM, tq), lambda i, h, t: (i, h, t, 0, 0))
    k_spec = pl.BlockSpec((1, 1, s, DIFF_DIM), lambda i, h, t: (i, h, 0, 0))
    return pl.pallas_call(
        functools.partial(_diff_attn_kernel, lam_init),
        grid=(b, nh, nq),
        in_specs=[
            qT_spec, qT_spec, k_spec, k_spec,
            pl.BlockSpec((1, 1, nq, DIFF_V, tq), lambda i, h, t: (i, h, 0, 0, 0)),
            _const_spec(lam_params.shape),
            _const_spec(subln.shape),
        ],
        out_specs=pl.BlockSpec((1, tq, DIFF_V), lambda i, h, t: (i, t, h)),
        out_shape=jax.ShapeDtypeStruct((b, s, nh * DIFF_V), BF16),
        scratch_shapes=[pltpu.VMEM((tq, tq), F32)] * 4 + [pltpu.VMEM((DIFF_V, tq), F32)] * 2,
        compiler_params=_params(3),
        name="diff_attn",
    )(q1T, q2T, k1, k2, vT, lam_params, subln)


def _out_proj_kernel(o_ref, om_ref, x_ref, w1_ref, w2_ref, g_ref, out_ref):
    mix = _dot(o_ref[0], w1_ref[...]) + _dot(om_ref[0], w2_ref[...])
    out_ref[0] = x_ref[0] + _rms(mix, g_ref[...])


def _out_proj(o, om, x, w1, w2, gain, tm):
    b, s, d = x.shape
    return pl.pallas_call(
        _out_proj_kernel,
        grid=(b, s // tm),
        in_specs=[
            pl.BlockSpec((1, tm, o.shape[-1]), lambda i, t: (i, t, 0)),
            pl.BlockSpec((1, tm, om.shape[-1]), lambda i, t: (i, t, 0)),
            pl.BlockSpec((1, tm, d), lambda i, t: (i, t, 0)),
            _const_spec(w1.shape),
            _const_spec(w2.shape),
            _const_spec(gain.shape),
        ],
        out_specs=pl.BlockSpec((1, tm, d), lambda i, t: (i, t, 0)),
        out_shape=jax.ShapeDtypeStruct((b, s, d), F32),
        compiler_params=_params(2),
        name="out_proj",
    )(o, om, x, w1, w2, gain)


def _gelu_tanh(g):
    c = math.sqrt(2.0 / math.pi)
    return 0.5 * g * (1.0 + jnp.tanh(c * (g + 0.044715 * (g * g * g))))


def _causal_conv(u, tail, cw, cb):
    ext = jnp.concatenate([tail, u], axis=0)
    u1 = pltpu.roll(ext, 1, 0)[SUBLANES:]
    u2 = pltpu.roll(ext, 2, 0)[SUBLANES:]
    return cb + cw[0:1] * u2 + cw[1:2] * u1 + cw[2:3] * u


def _conv_ffn_kernel(x_ref, pre_ref, wg_ref, wv_ref, cwg_ref, cwv_ref, cbg_ref, cbv_ref, wo_ref,
                     post_ref, out_ref, h_ref, acc_ref, tail_ref):
    t = pl.program_id(1)
    c = pl.program_id(2)
    nc = pl.num_programs(2)
    tm = x_ref.shape[1]

    @pl.when(c == 0)
    def _():
        h_ref[...] = _rms(x_ref[0], pre_ref[...]).astype(BF16)

    h = h_ref[...]
    ug = _dot(h, wg_ref[...])
    uv = _dot(h, wv_ref[...])

    @pl.when(t == 0)
    def _():
        tail_ref[c] = jnp.zeros(tail_ref.shape[1:], F32)

    tail_g = tail_ref[c, 0]
    tail_v = tail_ref[c, 1]
    tail_ref[c, 0] = ug[tm - SUBLANES:]
    tail_ref[c, 1] = uv[tm - SUBLANES:]
    cg = _causal_conv(ug, tail_g, cwg_ref[...], cbg_ref[...])
    cv = _causal_conv(uv, tail_v, cwv_ref[...], cbv_ref[...])
    y = (_gelu_tanh(cg) * cv).astype(BF16)
    part = _dot(y, wo_ref[...])

    @pl.when(c == 0)
    def _():
        acc_ref[...] = part

    @pl.when(c > 0)
    def _():
        acc_ref[...] += part

    @pl.when(c == nc - 1)
    def _():
        out_ref[0] = x_ref[0] + _rms(acc_ref[...], post_ref[...])


def _conv_ffn(x, pre, w_in, conv_w, conv_b, w_out, post, tm, fc):
    b, s, d = x.shape
    nc = D_FF // fc
    return pl.pallas_call(
        _conv_ffn_kernel,
        grid=(b, s // tm, nc),
        in_specs=[
            pl.BlockSpec((1, tm, d), lambda i, t, c: (i, t, 0)),
            _const_spec(pre.shape),
            pl.BlockSpec((d, fc), lambda i, t, c: (0, c)),
            pl.BlockSpec((d, fc), lambda i, t, c: (0, c + nc)),
            pl.BlockSpec((CONV_WIDTH, fc), lambda i, t, c: (0, c)),
            pl.BlockSpec((CONV_WIDTH, fc), lambda i, t, c: (0, c + nc)),
            pl.BlockSpec((1, fc), lambda i, t, c: (0, c)),
            pl.BlockSpec((1, fc), lambda i, t, c: (0, c + nc)),
            pl.BlockSpec((fc, d), lambda i, t, c: (c, 0)),
            _const_spec(post.shape),
        ],
        out_specs=pl.BlockSpec((1, tm, d), lambda i, t, c: (i, t, 0)),
        out_shape=jax.ShapeDtypeStruct((b, s, d), F32),
        scratch_shapes=[
            pltpu.VMEM((tm, d), BF16),
            pltpu.VMEM((tm, d), F32),
            pltpu.VMEM((nc, 2, SUBLANES, fc), F32),
        ],
        compiler_params=_params(3),
        name="conv_ffn",
    )(x, pre, w_in, w_in, conv_w, conv_w, conv_b, conv_b, w_out, post)


def _b_proj_kernel(x_ref, pos_ref, gs_ref, gb_ref, w_k_ref, w_vT_ref, w_qT_ref, w_qm_ref,
                   mkT_ref, mv_ref, q1T_ref, q2T_ref, k1_ref, k2_ref, vT_ref, om_ref):
    xf = x_ref[0]
    n = xf * lax.rsqrt(jnp.mean(xf * xf, axis=-1, keepdims=True) + NORM_EPS)
    hs = (n * gs_ref[...]).astype(BF16)
    hb = (n * gb_ref[...]).astype(BF16)

    half = DIFF_DIM // 2
    cos_r, sin_r = _rope_tables_rows(pos_ref[0], half, DIFF_DIM)

    tab_c = jnp.concatenate([cos_r, cos_r], axis=0).T
    tab_s = jnp.concatenate([-sin_r, sin_r], axis=0).T
    kk = _dot(hs, w_k_ref[...])
    vT = _dot_nt(w_vT_ref[...], hs)
    nk = DIFF_HEADS * DIFF_DIM
    for hd in range(DIFF_HEADS):
        for which, ref in ((0, k1_ref), (1, k2_ref)):
            base = which * nk + hd * DIFF_DIM
            xr = kk[:, base:base + DIFF_DIM]
            ref[0, hd] = (xr * tab_c + pltpu.roll(xr, half, 1) * tab_s).astype(BF16)
        vT_ref[0, hd, 0] = vT[hd * DIFF_V:(hd + 1) * DIFF_V].astype(BF16)

    scale = DIFF_DIM ** -0.5 * LOG2_E
    cos_q, sin_q = cos_r * scale, sin_r * scale
    qT = _dot_nt(w_qT_ref[...], hb)
    for hd in range(DIFF_HEADS):
        for which, ref in ((0, q1T_ref), (1, q2T_ref)):
            base = hd * 2 * DIFF_DIM + which * DIFF_DIM
            r1, r2 = _rope_rows(qT[base:base + half], qT[base + half:base + DIFF_DIM], cos_q, sin_q)
            ref[0, hd, 0] = jnp.concatenate([r1, r2], axis=0).astype(BF16)

    _mem_attention(_dot(hb, w_qm_ref[...]), mkT_ref, mv_ref, om_ref)


def _b_proj(x, pos_row, gs, gb, w_k, w_vT, w_qT, w_qm, mkT, mv, tm):
    b, s, d = x.shape
    nt = s // tm
    m = mkT.shape[-1]
    qT_shape = jax.ShapeDtypeStruct((b, DIFF_HEADS, nt, DIFF_DIM, tm), BF16)
    k_shape = jax.ShapeDtypeStruct((b, DIFF_HEADS, s, DIFF_DIM), BF16)
    qT_spec = pl.BlockSpec((1, DIFF_HEADS, 1, DIFF_DIM, tm), lambda i, t: (i, 0, t, 0, 0))
    k_spec = pl.BlockSpec((1, DIFF_HEADS, tm, DIFF_DIM), lambda i, t: (i, 0, t, 0))
    return pl.pallas_call(
        _b_proj_kernel,
        grid=(b, nt),
        in_specs=[
            pl.BlockSpec((1, tm, d), lambda i, t: (i, t, 0)),
            pl.BlockSpec((1, 1, tm), lambda i, t: (i, 0, t)),
            _const_spec(gs.shape),
            _const_spec(gb.shape),
            _const_spec(w_k.shape),
            _const_spec(w_vT.shape),
            _const_spec(w_qT.shape),
            _const_spec(w_qm.shape),
            pl.BlockSpec((1, 1, MEM_HEADS, MEM_DIM, m), lambda i, t: (1, i, 0, 0, 0)),
            pl.BlockSpec((1, 1, MEM_HEADS, m, MEM_DIM), lambda i, t: (1, i, 0, 0, 0)),
        ],
        out_specs=[
            qT_spec, qT_spec, k_spec, k_spec,
            pl.BlockSpec((1, DIFF_HEADS, 1, DIFF_V, tm), lambda i, t: (i, 0, t, 0, 0)),
            pl.BlockSpec((1, tm, MEM_HEADS * MEM_DIM), lambda i, t: (i, t, 0)),
        ],
        out_shape=[
            qT_shape, qT_shape, k_shape, k_shape,
            jax.ShapeDtypeStruct((b, DIFF_HEADS, nt, DIFF_V, tm), BF16),
            jax.ShapeDtypeStruct((b, s, MEM_HEADS * MEM_DIM), BF16),
        ],
        compiler_params=_params(2),
        name="b_proj",
    )(x, pos_row, gs, gb, w_k, w_vT, w_qT, w_qm, mkT, mv)


def _row(v):
    return v.reshape(1, -1).astype(F32)


def kernel(x, mem, positions, a_attn_norm_pre, a_w_in, a_q_norm, a_w_uq, a_kv_norm, a_w_ukv, a_mem_norm, a_w_mem_kv, a_w_o, a_attn_norm_post, shared_kv_norm, shared_w_kv, b_attn_norm_pre, b_w_in, b_lambda_q1, b_lambda_k1, b_lambda_q2, b_lambda_k2, b_subln, b_mem_norm, b_w_mem_kv, b_w_o, b_attn_norm_post, ffn_norm_pre, ffn_w_in, ffn_conv_w, ffn_conv_b, ffn_w_out, ffn_norm_post):
    b, s, d = x.shape
    assert d == D_MODEL and a_attn_norm_pre.shape[0] == 1 and b_attn_norm_pre.shape[0] == 1
    tm, tq, fc = _tiles(s)
    assert s % tm == 0 and s % tq == 0 and tm == tq and D_FF % fc == 0
    pos_row = positions.reshape(b, 1, s)

    mem_gains = jnp.stack([a_mem_norm[0], b_mem_norm[0]]).reshape(2, 1, d)
    mem_w = jnp.stack([a_w_mem_kv[0], b_w_mem_kv[0]]).astype(BF16)
    mkT, mv = _mem_kv(mem, mem_gains, mem_w)

    w_in = a_w_in[0]
    i2 = Q_LORA + KV_LORA + MLA_ROPE
    w_in_p = jnp.concatenate(
        [w_in[:, :i2], jnp.zeros((d, LANES - MLA_ROPE), F32), w_in[:, i2:]], axis=1).astype(BF16)
    w_uqT = a_w_uq[0].T.astype(BF16)
    w_ukv = a_w_ukv[0].reshape(KV_LORA, MLA_HEADS, MLA_NOPE + MLA_V)
    w_k = w_ukv[:, :, :MLA_NOPE].reshape(KV_LORA, MLA_HEADS * MLA_NOPE).astype(BF16)
    w_vT = w_ukv[:, :, MLA_NOPE:].reshape(KV_LORA, MLA_HEADS * MLA_V).T.astype(BF16)
    qT, k, vT, om = _a_proj(x, pos_row, _row(a_attn_norm_pre[0]), w_in_p, _row(a_q_norm[0]), w_uqT,
                            _row(a_kv_norm[0]), w_k, w_vT, mkT, mv, tm)
    o = _mla_attn(qT, k, vT, tq)
    n_main = MLA_HEADS * MLA_V
    w_o = a_w_o[0].astype(BF16)
    x = _out_proj(o, om, x, w_o[:n_main], w_o[n_main:], _row(a_attn_norm_post[0]), tm)
    x = _conv_ffn(x, _row(ffn_norm_pre[0]), ffn_w_in[0].astype(BF16), ffn_conv_w[0],
                  _row(ffn_conv_b[0]), ffn_w_out[0].astype(BF16), _row(ffn_norm_post[0]), tm, fc)

    layer = N_A_LAYERS
    lam_init = 0.8 - 0.6 * math.exp(-0.3 * layer)
    w_kv = shared_w_kv.reshape(d, DIFF_HEADS, 2 * DIFF_DIM + DIFF_V)
    nk = DIFF_HEADS * DIFF_DIM
    w_k12 = jnp.concatenate(
        [w_kv[:, :, :DIFF_DIM].reshape(d, nk), w_kv[:, :, DIFF_DIM:2 * DIFF_DIM].reshape(d, nk)],
        axis=1).astype(BF16)
    w_svT = w_kv[:, :, 2 * DIFF_DIM:].reshape(d, DIFF_HEADS * DIFF_V).T.astype(BF16)
    nq_cols = DIFF_HEADS * 2 * DIFF_DIM
    w_bq_T = b_w_in[0][:, :nq_cols].T.astype(BF16)
    w_bqm = b_w_in[0][:, nq_cols:].astype(BF16)
    q1T, q2T, k1, k2, vbT, omb = _b_proj(x, pos_row, _row(shared_kv_norm), _row(b_attn_norm_pre[0]),
                                         w_k12, w_svT, w_bq_T, w_bqm, mkT, mv, tm)
    lam_params = jnp.stack([b_lambda_q1[0], b_lambda_k1[0], b_lambda_q2[0], b_lambda_k2[0]]).astype(F32)
    ob = _diff_attn(q1T, q2T, k1, k2, vbT, lam_params, _row(b_subln[0]), lam_init, tq)
    n_main_b = DIFF_HEADS * DIFF_V
    w_ob = b_w_o[0].astype(BF16)
    x = _out_proj(ob, omb, x, w_ob[:n_main_b], w_ob[n_main_b:], _row(b_attn_norm_post[0]), tm)
    x = _conv_ffn(x, _row(ffn_norm_pre[1]), ffn_w_in[1].astype(BF16), ffn_conv_w[1],
                  _row(ffn_conv_b[1]), ffn_w_out[1].astype(BF16), _row(ffn_norm_post[1]), tm, fc)
    return x
```

```python
import functools
import math

import jax
import jax.numpy as jnp
from jax import lax
from jax.experimental import pallas as pl
from jax.experimental.pallas import tpu as pltpu

D_MODEL = 2048
ROPE_THETA = 10000.0
NORM_EPS = 1e-6
MLA_HEADS = 12
MLA_NOPE = 128
MLA_ROPE = 64
MLA_V = 128
MLA_DK = MLA_NOPE + MLA_ROPE
Q_LORA = 512
KV_LORA = 512
DIFF_HEADS = 6
DIFF_DIM = 128
DIFF_V = 2 * DIFF_DIM
MEM_HEADS = 4
MEM_DIM = 128
D_FF = 5632
CONV_WIDTH = 3
N_A_LAYERS = 1

LANES = 128
SUBLANES = 8
A_IN_PAD = Q_LORA + KV_LORA + LANES + MEM_HEADS * MEM_DIM

V7X_VMEM_BYTES = 64 * 1024 * 1024
VMEM_LIMIT = V7X_VMEM_BYTES - 8 * 1024 * 1024

NEG_BIG = -1e30
LOG2_E = math.log2(math.e)
BF16 = jnp.bfloat16
F32 = jnp.float32


def _tiles(seq):
    tm = min(512, seq)
    tq = min(512, seq)
    fc = 512
    return tm, tq, fc


def _params(n_axes):
    return pltpu.CompilerParams(dimension_semantics=("arbitrary",) * n_axes,
                                vmem_limit_bytes=VMEM_LIMIT)


def _const_spec(shape):
    nd = len(shape)
    return pl.BlockSpec(shape, lambda *_: (0,) * nd, pipeline_mode=pl.Buffered(1))


def _rms(xf, gain):
    ms = jnp.mean(xf * xf, axis=-1, keepdims=True)
    return xf * lax.rsqrt(ms + NORM_EPS) * gain


def _dot(a, b):
    return jnp.dot(a, b, preferred_element_type=F32)


def _dot_nt(a, b):
    return lax.dot_general(a, b, (((1,), (1,)), ((), ())), preferred_element_type=F32)


def _rope_tables_rows(pos_row, half, dh):
    tokens = pos_row.shape[-1]
    j = lax.broadcasted_iota(jnp.int32, (half, tokens), 0).astype(F32)
    inv_freq = jnp.exp(j * (-math.log(ROPE_THETA) * 2.0 / dh))
    ang = pos_row.astype(F32) * inv_freq
    return jnp.cos(ang), jnp.sin(ang)


def _rope_rows(a, b, cos_r, sin_r):
    return a * cos_r - b * sin_r, b * cos_r + a * sin_r


def _mem_attention(qm, mkT_ref, mv_ref, out_ref):
    scale = MEM_DIM ** -0.5
    for h in range(MEM_HEADS):
        qh = (qm[:, h * MEM_DIM:(h + 1) * MEM_DIM] * scale).astype(BF16)
        s = _dot(qh, mkT_ref[0, 0, h])
        m = jnp.max(s, axis=-1, keepdims=True)
        p = jnp.exp(s - m)
        l = jnp.sum(p, axis=-1, keepdims=True)
        o = _dot(p.astype(BF16), mv_ref[0, 0, h])
        out_ref[0, :, h * MEM_DIM:(h + 1) * MEM_DIM] = (o / l).astype(out_ref.dtype)


def _mem_kv_kernel(mem_ref, g_ref, w_ref, kT_ref, v_ref):
    n = _rms(mem_ref[0], g_ref[0]).astype(BF16)
    kv = _dot(n, w_ref[0])
    for h in range(MEM_HEADS):
        base = h * 2 * MEM_DIM
        kT_ref[0, 0, h] = kv[:, base:base + MEM_DIM].T.astype(BF16)
        v_ref[0, 0, h] = kv[:, base + MEM_DIM:base + 2 * MEM_DIM].astype(BF16)


def _mem_kv(mem, gains, weights):
    b, m, d = mem.shape
    nl = gains.shape[0]
    return pl.pallas_call(
        _mem_kv_kernel,
        grid=(nl, b),
        in_specs=[
            pl.BlockSpec((1, m, d), lambda l, i: (i, 0, 0)),
            pl.BlockSpec((1, 1, d), lambda l, i: (l, 0, 0)),
            pl.BlockSpec((1, d, MEM_HEADS * 2 * MEM_DIM), lambda l, i: (l, 0, 0)),
        ],
        out_specs=[
            pl.BlockSpec((1, 1, MEM_HEADS, MEM_DIM, m), lambda l, i: (l, i, 0, 0, 0)),
            pl.BlockSpec((1, 1, MEM_HEADS, m, MEM_DIM), lambda l, i: (l, i, 0, 0, 0)),
        ],
        out_shape=[
            jax.ShapeDtypeStruct((nl, b, MEM_HEADS, MEM_DIM, m), BF16),
            jax.ShapeDtypeStruct((nl, b, MEM_HEADS, m, MEM_DIM), BF16),
        ],
        compiler_params=_params(2),
        name="mem_kv",
    )(mem, gains, weights)


def _a_proj_kernel(x_ref, pos_ref, g_ref, w_in_ref, qg_ref, w_uqT_ref, kvg_ref, w_k_ref, w_vT_ref,
                   mkT_ref, mv_ref, qT_ref, k_ref, vT_ref, om_ref):
    tm = x_ref.shape[1]
    h = _rms(x_ref[0], g_ref[...]).astype(BF16)
    proj = _dot(h, w_in_ref[...])
    ql = _rms(proj[:, :Q_LORA], qg_ref[...]).astype(BF16)
    kvl = _rms(proj[:, Q_LORA:Q_LORA + KV_LORA], kvg_ref[...]).astype(BF16)
    pe = proj[:, Q_LORA + KV_LORA:Q_LORA + KV_LORA + LANES]
    qm = proj[:, Q_LORA + KV_LORA + LANES:]

    half = MLA_ROPE // 2
    cos_r, sin_r = _rope_tables_rows(pos_ref[0], half, MLA_ROPE)

    peT = pe.T
    r1, r2 = _rope_rows(peT[0:half], peT[half:2 * half], cos_r, sin_r)
    kpe = jnp.concatenate([r1, r2, jnp.zeros((LANES - MLA_ROPE, tm), F32)], axis=0).T
    kpe = kpe[:, :MLA_ROPE].astype(BF16)

    scale = MLA_DK ** -0.5 * LOG2_E
    cos_q, sin_q = cos_r * scale, sin_r * scale
    qT = _dot_nt(w_uqT_ref[...], ql)
    kn = _dot(kvl, w_k_ref[...])
    vT = _dot_nt(w_vT_ref[...], kvl)
    for hd in range(MLA_HEADS):
        base = hd * MLA_DK
        qT_ref[0, hd, 0, 0:MLA_NOPE, :] = (qT[base:base + MLA_NOPE] * scale).astype(BF16)
        r1, r2 = _rope_rows(qT[base + MLA_NOPE:base + MLA_NOPE + half],
                            qT[base + MLA_NOPE + half:base + MLA_DK], cos_q, sin_q)
        qT_ref[0, hd, 0, MLA_NOPE:MLA_NOPE + half, :] = r1.astype(BF16)
        qT_ref[0, hd, 0, MLA_NOPE + half:MLA_DK, :] = r2.astype(BF16)
        k_ref[0, hd, :, 0:MLA_NOPE] = kn[:, hd * MLA_NOPE:(hd + 1) * MLA_NOPE].astype(BF16)
        k_ref[0, hd, :, MLA_NOPE:MLA_DK] = kpe
        vT_ref[0, hd, 0] = vT[hd * MLA_V:(hd + 1) * MLA_V].astype(BF16)

    _mem_attention(qm, mkT_ref, mv_ref, om_ref)


def _a_proj(x, pos_row, gain, w_in, qg, w_uqT, kvg, w_k, w_vT, mkT, mv, tm):
    b, s, d = x.shape
    nt = s // tm
    m = mkT.shape[-1]
    return pl.pallas_call(
        _a_proj_kernel,
        grid=(b, nt),
        in_specs=[
            pl.BlockSpec((1, tm, d), lambda i, t: (i, t, 0)),
            pl.BlockSpec((1, 1, tm), lambda i, t: (i, 0, t)),
            _const_spec(gain.shape),
            _const_spec(w_in.shape),
            _const_spec(qg.shape),
            _const_spec(w_uqT.shape),
            _const_spec(kvg.shape),
            _const_spec(w_k.shape),
            _const_spec(w_vT.shape),
            pl.BlockSpec((1, 1, MEM_HEADS, MEM_DIM, m), lambda i, t: (0, i, 0, 0, 0)),
            pl.BlockSpec((1, 1, MEM_HEADS, m, MEM_DIM), lambda i, t: (0, i, 0, 0, 0)),
        ],
        out_specs=[
            pl.BlockSpec((1, MLA_HEADS, 1, MLA_DK, tm), lambda i, t: (i, 0, t, 0, 0)),
            pl.BlockSpec((1, MLA_HEADS, tm, MLA_DK), lambda i, t: (i, 0, t, 0)),
            pl.BlockSpec((1, MLA_HEADS, 1, MLA_V, tm), lambda i, t: (i, 0, t, 0, 0)),
            pl.BlockSpec((1, tm, MEM_HEADS * MEM_DIM), lambda i, t: (i, t, 0)),
        ],
        out_shape=[
            jax.ShapeDtypeStruct((b, MLA_HEADS, nt, MLA_DK, tm), BF16),
            jax.ShapeDtypeStruct((b, MLA_HEADS, s, MLA_DK), BF16),
            jax.ShapeDtypeStruct((b, MLA_HEADS, nt, MLA_V, tm), BF16),
            jax.ShapeDtypeStruct((b, s, MEM_HEADS * MEM_DIM), BF16),
        ],
        compiler_params=_params(2),
        name="a_proj",
    )(x, pos_row, gain, w_in, qg, w_uqT, kvg, w_k, w_vT, mkT, mv)


def _causal_mask_t(tq):
    key = lax.broadcasted_iota(jnp.int32, (tq, tq), 0)
    qry = lax.broadcasted_iota(jnp.int32, (tq, tq), 1)
    return key <= qry


def _online_softmax_t(s, m_prev, l_prev, mask):
    if mask is not None:
        s = jnp.where(mask, s, NEG_BIG)
    m_new = jnp.maximum(m_prev, jnp.max(s, axis=0, keepdims=True))
    alpha = jnp.exp2(m_prev - m_new)
    p = jnp.exp2(s - m_new)
    l_new = alpha * l_prev + jnp.sum(p, axis=0, keepdims=True)
    return p.astype(BF16), m_new, l_new, alpha


KEY_BLOCK_UNROLL = 4


def _pipelined_key_blocks(qi, scores, consume, finish, init, mask):
    scores(0, 0)

    def group(g, carry):
        j = KEY_BLOCK_UNROLL * g
        for u in range(KEY_BLOCK_UNROLL):
            scores(j + u + 1, (u + 1) % 2)
            carry = consume(u % 2, j + u, carry, None)
        return carry

    carry = lax.fori_loop(0, qi // KEY_BLOCK_UNROLL, group, init)
    base = (qi // KEY_BLOCK_UNROLL) * KEY_BLOCK_UNROLL

    for rem in range(KEY_BLOCK_UNROLL):
        @pl.when(qi % KEY_BLOCK_UNROLL == rem)
        def _(rem=rem):
            c = carry
            for u in range(rem):
                scores(base + u + 1, (u + 1) % 2)
                c = consume(u % 2, base + u, c, None)
            finish(consume(rem % 2, qi, c, mask))


def _mla_attn_kernel(qT_ref, k_ref, vT_ref, o_ref, sa_ref, sb_ref, acc_ref):
    tq = qT_ref.shape[4]
    qi = pl.program_id(2)
    qT = qT_ref[0, 0, 0]
    s_refs = (sa_ref, sb_ref)

    def scores(j, slot):
        off = pl.multiple_of(j * tq, tq)
        s_refs[slot][...] = _dot(k_ref[0, 0, pl.ds(off, tq), :], qT)

    def consume(slot, j, carry, mask):
        m_prev, l_prev = carry
        p, m_new, l_new, alpha = _online_softmax_t(s_refs[slot][...], m_prev, l_prev, mask)
        acc_ref[...] = alpha * acc_ref[...] + _dot(vT_ref[0, 0, j], p)
        return m_new, l_new

    def finish(carry):
        _, l = carry
        o_ref[0] = (acc_ref[...] / l).T.astype(o_ref.dtype)

    acc_ref[...] = jnp.zeros(acc_ref.shape, F32)
    init = (jnp.full((1, tq), NEG_BIG, F32), jnp.zeros((1, tq), F32))
    _pipelined_key_blocks(qi, scores, consume, finish, init, _causal_mask_t(tq))


def _mla_attn(qT, k, vT, tq):
    b, nh, nq, _, _ = qT.shape
    s = nq * tq
    return pl.pallas_call(
        _mla_attn_kernel,
        grid=(b, nh, nq),
        in_specs=[
            pl.BlockSpec((1, 1, 1, MLA_DK, tq), lambda i, h, t: (i, h, t, 0, 0)),
            pl.BlockSpec((1, 1, s, MLA_DK), lambda i, h, t: (i, h, 0, 0)),
            pl.BlockSpec((1, 1, nq, MLA_V, tq), lambda i, h, t: (i, h, 0, 0, 0)),
        ],
        out_specs=pl.BlockSpec((1, tq, MLA_V), lambda i, h, t: (i, t, h)),
        out_shape=jax.ShapeDtypeStruct((b, s, nh * MLA_V), BF16),
        scratch_shapes=[pltpu.VMEM((tq, tq), F32), pltpu.VMEM((tq, tq), F32),
                        pltpu.VMEM((MLA_V, tq), F32)],
        compiler_params=_params(3),
        name="mla_attn",
    )(qT, k, vT)


def _diff_attn_kernel(lam_init, q1T_ref, q2T_ref, k1_ref, k2_ref, vT_ref, lam_ref, sub_ref, o_ref,
                      s1a_ref, s1b_ref, s2a_ref, s2b_ref, acc1_ref, acc2_ref):
    tq = q1T_ref.shape[4]
    qi = pl.program_id(2)
    q1T = q1T_ref[0, 0, 0]
    q2T = q2T_ref[0, 0, 0]
    s1_refs = (s1a_ref, s1b_ref)
    s2_refs = (s2a_ref, s2b_ref)

    def scores(j, slot):
        off = pl.multiple_of(j * tq, tq)
        s1_refs[slot][...] = _dot(k1_ref[0, 0, pl.ds(off, tq), :], q1T)
        s2_refs[slot][...] = _dot(k2_ref[0, 0, pl.ds(off, tq), :], q2T)

    def consume(slot, j, carry, mask):
        m1, l1, m2, l2 = carry
        vT = vT_ref[0, 0, j]
        p1, m1, l1, a1 = _online_softmax_t(s1_refs[slot][...], m1, l1, mask)
        acc1_ref[...] = a1 * acc1_ref[...] + _dot(vT, p1)
        p2, m2, l2, a2 = _online_softmax_t(s2_refs[slot][...], m2, l2, mask)
        acc2_ref[...] = a2 * acc2_ref[...] + _dot(vT, p2)
        return m1, l1, m2, l2

    def finish(carry):
        _, l1, _, l2 = carry
        lp = lam_ref[...]
        lam = (jnp.exp(jnp.sum(lp[0:1] * lp[1:2], axis=-1, keepdims=True))
               - jnp.exp(jnp.sum(lp[2:3] * lp[3:4], axis=-1, keepdims=True)) + lam_init)
        oT = acc1_ref[...] / l1 - lam * (acc2_ref[...] / l2)
        o_ref[0] = (_rms(oT.T, sub_ref[...]) * (1.0 - lam_init)).astype(o_ref.dtype)

    acc1_ref[...] = jnp.zeros(acc1_ref.shape, F32)
    acc2_ref[...] = jnp.zeros(acc2_ref.shape, F32)
    m_init = jnp.full((1, tq), NEG_BIG, F32)
    l_init = jnp.zeros((1, tq), F32)
    _pipelined_key_blocks(qi, scores, consume, finish, (m_init, l_init, m_init, l_init),
                          _causal_mask_t(tq))


def _diff_attn(q1T, q2T, k1, k2, vT, lam_params, subln, lam_init, tq):
    b, nh, nq, _, _ = q1T.shape
    s = nq * tq
    qT_spec = pl.BlockSpec((1, 1, 1, DIFF_DIM, tq), lambda i, h, t: (i, h, t, 0, 0))
    k_spec = pl.BlockSpec((1, 1, s, DIFF_DIM), lambda i, h, t: (i, h, 0, 0))
    return pl.pallas_call(
        functools.partial(_diff_attn_kernel, lam_init),
        grid=(b, nh, nq),
        in_specs=[
            qT_spec, qT_spec, k_spec, k_spec,
            pl.BlockSpec((1, 1, nq, DIFF_V, tq), lambda i, h, t: (i, h, 0, 0, 0)),
            _const_spec(lam_params.shape),
            _const_spec(subln.shape),
        ],
        out_specs=pl.BlockSpec((1, tq, DIFF_V), lambda i, h, t: (i, t, h)),
        out_shape=jax.ShapeDtypeStruct((b, s, nh * DIFF_V), BF16),
        scratch_shapes=[pltpu.VMEM((tq, tq), F32)] * 4 + [pltpu.VMEM((DIFF_V, tq), F32)] * 2,
        compiler_params=_params(3),
        name="diff_attn",
    )(q1T, q2T, k1, k2, vT, lam_params, subln)


def _out_proj_kernel(o_ref, om_ref, x_ref, w1_ref, w2_ref, g_ref, out_ref):
    mix = _dot(o_ref[0], w1_ref[...]) + _dot(om_ref[0], w2_ref[...])
    out_ref[0] = x_ref[0] + _rms(mix, g_ref[...])


def _out_proj(o, om, x, w1, w2, gain, tm):
    b, s, d = x.shape
    return pl.pallas_call(
        _out_proj_kernel,
        grid=(b, s // tm),
        in_specs=[
            pl.BlockSpec((1, tm, o.shape[-1]), lambda i, t: (i, t, 0)),
            pl.BlockSpec((1, tm, om.shape[-1]), lambda i, t: (i, t, 0)),
            pl.BlockSpec((1, tm, d), lambda i, t: (i, t, 0)),
            _const_spec(w1.shape),
            _const_spec(w2.shape),
            _const_spec(gain.shape),
        ],
        out_specs=pl.BlockSpec((1, tm, d), lambda i, t: (i, t, 0)),
        out_shape=jax.ShapeDtypeStruct((b, s, d), F32),
        compiler_params=_params(2),
        name="out_proj",
    )(o, om, x, w1, w2, gain)


FFN_SUB = 256
FFN_ROWS = 256


def _gelu_tanh(g):
    c = math.sqrt(2.0 / math.pi)
    return 0.5 * g * (1.0 + jnp.tanh(c * (g + 0.044715 * (g * g * g))))


def _causal_conv(u_ref, r, cw, cb):
    base = SUBLANES + r
    u0 = u_ref[base:base + FFN_ROWS, :]
    u1 = u_ref[base - 1:base - 1 + FFN_ROWS, :]
    u2 = u_ref[base - 2:base - 2 + FFN_ROWS, :]
    return cb + cw[0:1] * u2 + cw[1:2] * u1 + cw[2:3] * u0


def _conv_ffn_kernel(x_ref, pre_ref, wg_ref, wv_ref, cwg_ref, cwv_ref, cbg_ref, cbv_ref, wo_ref,
                     post_ref, out_ref, h_ref, acc_ref, tail_ref, u_ref):
    t = pl.program_id(1)
    c = pl.program_id(2)
    nc = pl.num_programs(2)
    tm = x_ref.shape[1]
    fc = wg_ref.shape[1]

    @pl.when(c == 0)
    def _():
        h_ref[...] = _rms(x_ref[0], pre_ref[...]).astype(BF16)
        acc_ref[...] = jnp.zeros(acc_ref.shape, F32)

    @pl.when(t == 0)
    def _():
        tail_ref[c] = jnp.zeros(tail_ref.shape[1:], F32)

    subs = list(range(0, fc, FFN_SUB))
    for si, lo in enumerate(subs):
        u_ref[si, 0, 0:SUBLANES, :] = tail_ref[c, 0, :, lo:lo + FFN_SUB]
        u_ref[si, 1, 0:SUBLANES, :] = tail_ref[c, 1, :, lo:lo + FFN_SUB]

    groups = [(si, r) for si in range(len(subs)) for r in range(0, tm, FFN_ROWS)]

    def up(si, r):
        lo = subs[si]
        hr = h_ref[r:r + FFN_ROWS, :]
        rows = slice(SUBLANES + r, SUBLANES + r + FFN_ROWS)
        u_ref[si, 0, rows, :] = _dot(hr, wg_ref[:, lo:lo + FFN_SUB])
        u_ref[si, 1, rows, :] = _dot(hr, wv_ref[:, lo:lo + FFN_SUB])

    up(*groups[0])
    for i, (si, r) in enumerate(groups):
        lo, hi = subs[si], subs[si] + FFN_SUB
        if i + 1 < len(groups):
            up(*groups[i + 1])
        cg = _causal_conv(u_ref.at[si, 0], r, cwg_ref[:, lo:hi], cbg_ref[:, lo:hi])
        cv = _causal_conv(u_ref.at[si, 1], r, cwv_ref[:, lo:hi], cbv_ref[:, lo:hi])
        y = (_gelu_tanh(cg) * cv).astype(BF16)
        acc_ref[r:r + FFN_ROWS, :] += _dot(y, wo_ref[lo:hi, :])

    for si, lo in enumerate(subs):
        tail_ref[c, 0, :, lo:lo + FFN_SUB] = u_ref[si, 0, tm:tm + SUBLANES, :]
        tail_ref[c, 1, :, lo:lo + FFN_SUB] = u_ref[si, 1, tm:tm + SUBLANES, :]

    @pl.when(c == nc - 1)
    def _():
        out_ref[0] = x_ref[0] + _rms(acc_ref[...], post_ref[...])


def _conv_ffn(x, pre, w_in, conv_w, conv_b, w_out, post, tm, fc):
    b, s, d = x.shape
    nc = D_FF // fc
    return pl.pallas_call(
        _conv_ffn_kernel,
        grid=(b, s // tm, nc),
        in_specs=[
            pl.BlockSpec((1, tm, d), lambda i, t, c: (i, t, 0)),
            _const_spec(pre.shape),
            pl.BlockSpec((d, fc), lambda i, t, c: (0, c)),
            pl.BlockSpec((d, fc), lambda i, t, c: (0, c + nc)),
            pl.BlockSpec((CONV_WIDTH, fc), lambda i, t, c: (0, c)),
            pl.BlockSpec((CONV_WIDTH, fc), lambda i, t, c: (0, c + nc)),
            pl.BlockSpec((1, fc), lambda i, t, c: (0, c)),
            pl.BlockSpec((1, fc), lambda i, t, c: (0, c + nc)),
            pl.BlockSpec((fc, d), lambda i, t, c: (c, 0)),
            _const_spec(post.shape),
        ],
        out_specs=pl.BlockSpec((1, tm, d), lambda i, t, c: (i, t, 0)),
        out_shape=jax.ShapeDtypeStruct((b, s, d), F32),
        scratch_shapes=[
            pltpu.VMEM((tm, d), BF16),
            pltpu.VMEM((tm, d), F32),
            pltpu.VMEM((nc, 2, SUBLANES, fc), F32),
            pltpu.VMEM((fc // FFN_SUB, 2, SUBLANES + tm, FFN_SUB), F32),
        ],
        compiler_params=_params(3),
        name="conv_ffn",
    )(x, pre, w_in, w_in, conv_w, conv_w, conv_b, conv_b, w_out, post)


def _b_proj_kernel(x_ref, pos_ref, gs_ref, gb_ref, w_k_ref, w_vT_ref, w_qT_ref, w_qm_ref,
                   mkT_ref, mv_ref, q1T_ref, q2T_ref, k1_ref, k2_ref, vT_ref, om_ref):
    xf = x_ref[0]
    n = xf * lax.rsqrt(jnp.mean(xf * xf, axis=-1, keepdims=True) + NORM_EPS)
    hs = (n * gs_ref[...]).astype(BF16)
    hb = (n * gb_ref[...]).astype(BF16)

    half = DIFF_DIM // 2
    cos_r, sin_r = _rope_tables_rows(pos_ref[0], half, DIFF_DIM)

    tab_c = jnp.concatenate([cos_r, cos_r], axis=0).T
    tab_s = jnp.concatenate([-sin_r, sin_r], axis=0).T
    kk = _dot(hs, w_k_ref[...])
    vT = _dot_nt(w_vT_ref[...], hs)
    nk = DIFF_HEADS * DIFF_DIM
    for hd in range(DIFF_HEADS):
        for which, ref in ((0, k1_ref), (1, k2_ref)):
            base = which * nk + hd * DIFF_DIM
            xr = kk[:, base:base + DIFF_DIM]
            ref[0, hd] = (xr * tab_c + pltpu.roll(xr, half, 1) * tab_s).astype(BF16)
        vT_ref[0, hd, 0] = vT[hd * DIFF_V:(hd + 1) * DIFF_V].astype(BF16)

    scale = DIFF_DIM ** -0.5 * LOG2_E
    cos_q, sin_q = cos_r * scale, sin_r * scale
    qT = _dot_nt(w_qT_ref[...], hb)
    for hd in range(DIFF_HEADS):
        for which, ref in ((0, q1T_ref), (1, q2T_ref)):
            base = hd * 2 * DIFF_DIM + which * DIFF_DIM
            r1, r2 = _rope_rows(qT[base:base + half], qT[base + half:base + DIFF_DIM], cos_q, sin_q)
            ref[0, hd, 0] = jnp.concatenate([r1, r2], axis=0).astype(BF16)

    _mem_attention(_dot(hb, w_qm_ref[...]), mkT_ref, mv_ref, om_ref)


def _b_proj(x, pos_row, gs, gb, w_k, w_vT, w_qT, w_qm, mkT, mv, tm):
    b, s, d = x.shape
    nt = s // tm
    m = mkT.shape[-1]
    qT_shape = jax.ShapeDtypeStruct((b, DIFF_HEADS, nt, DIFF_DIM, tm), BF16)
    k_shape = jax.ShapeDtypeStruct((b, DIFF_HEADS, s, DIFF_DIM), BF16)
    qT_spec = pl.BlockSpec((1, DIFF_HEADS, 1, DIFF_DIM, tm), lambda i, t: (i, 0, t, 0, 0))
    k_spec = pl.BlockSpec((1, DIFF_HEADS, tm, DIFF_DIM), lambda i, t: (i, 0, t, 0))
    return pl.pallas_call(
        _b_proj_kernel,
        grid=(b, nt),
        in_specs=[
            pl.BlockSpec((1, tm, d), lambda i, t: (i, t, 0)),
            pl.BlockSpec((1, 1, tm), lambda i, t: (i, 0, t)),
            _const_spec(gs.shape),
            _const_spec(gb.shape),
            _const_spec(w_k.shape),
            _const_spec(w_vT.shape),
            _const_spec(w_qT.shape),
            _const_spec(w_qm.shape),
            pl.BlockSpec((1, 1, MEM_HEADS, MEM_DIM, m), lambda i, t: (1, i, 0, 0, 0)),
            pl.BlockSpec((1, 1, MEM_HEADS, m, MEM_DIM), lambda i, t: (1, i, 0, 0, 0)),
        ],
        out_specs=[
            qT_spec, qT_spec, k_spec, k_spec,
            pl.BlockSpec((1, DIFF_HEADS, 1, DIFF_V, tm), lambda i, t: (i, 0, t, 0, 0)),
            pl.BlockSpec((1, tm, MEM_HEADS * MEM_DIM), lambda i, t: (i, t, 0)),
        ],
        out_shape=[
            qT_shape, qT_shape, k_shape, k_shape,
            jax.ShapeDtypeStruct((b, DIFF_HEADS, nt, DIFF_V, tm), BF16),
            jax.ShapeDtypeStruct((b, s, MEM_HEADS * MEM_DIM), BF16),
        ],
        compiler_params=_params(2),
        name="b_proj",
    )(x, pos_row, gs, gb, w_k, w_vT, w_qT, w_qm, mkT, mv)


def _row(v):
    return v.reshape(1, -1).astype(F32)


def kernel(x, mem, positions, a_attn_norm_pre, a_w_in, a_q_norm, a_w_uq, a_kv_norm, a_w_ukv, a_mem_norm, a_w_mem_kv, a_w_o, a_attn_norm_post, shared_kv_norm, shared_w_kv, b_attn_norm_pre, b_w_in, b_lambda_q1, b_lambda_k1, b_lambda_q2, b_lambda_k2, b_subln, b_mem_norm, b_w_mem_kv, b_w_o, b_attn_norm_post, ffn_norm_pre, ffn_w_in, ffn_conv_w, ffn_conv_b, ffn_w_out, ffn_norm_post):
    b, s, d = x.shape
    assert d == D_MODEL and a_attn_norm_pre.shape[0] == 1 and b_attn_norm_pre.shape[0] == 1
    tm, tq, fc = _tiles(s)
    assert s % tm == 0 and s % tq == 0 and tm == tq and D_FF % fc == 0
    pos_row = positions.reshape(b, 1, s)

    mem_gains = jnp.stack([a_mem_norm[0], b_mem_norm[0]]).reshape(2, 1, d)
    mem_w = jnp.stack([a_w_mem_kv[0], b_w_mem_kv[0]]).astype(BF16)
    mkT, mv = _mem_kv(mem, mem_gains, mem_w)

    w_in = a_w_in[0]
    i2 = Q_LORA + KV_LORA + MLA_ROPE
    w_in_p = jnp.concatenate(
        [w_in[:, :i2], jnp.zeros((d, LANES - MLA_ROPE), F32), w_in[:, i2:]], axis=1).astype(BF16)
    w_uqT = a_w_uq[0].T.astype(BF16)
    w_ukv = a_w_ukv[0].reshape(KV_LORA, MLA_HEADS, MLA_NOPE + MLA_V)
    w_k = w_ukv[:, :, :MLA_NOPE].reshape(KV_LORA, MLA_HEADS * MLA_NOPE).astype(BF16)
    w_vT = w_ukv[:, :, MLA_NOPE:].reshape(KV_LORA, MLA_HEADS * MLA_V).T.astype(BF16)
    qT, k, vT, om = _a_proj(x, pos_row, _row(a_attn_norm_pre[0]), w_in_p, _row(a_q_norm[0]), w_uqT,
                            _row(a_kv_norm[0]), w_k, w_vT, mkT, mv, tm)
    o = _mla_attn(qT, k, vT, tq)
    n_main = MLA_HEADS * MLA_V
    w_o = a_w_o[0].astype(BF16)
    x = _out_proj(o, om, x, w_o[:n_main], w_o[n_main:], _row(a_attn_norm_post[0]), tm)
    x = _conv_ffn(x, _row(ffn_norm_pre[0]), ffn_w_in[0].astype(BF16), ffn_conv_w[0],
                  _row(ffn_conv_b[0]), ffn_w_out[0].astype(BF16), _row(ffn_norm_post[0]), tm, fc)

    layer = N_A_LAYERS
    lam_init = 0.8 - 0.6 * math.exp(-0.3 * layer)
    w_kv = shared_w_kv.reshape(d, DIFF_HEADS, 2 * DIFF_DIM + DIFF_V)
    nk = DIFF_HEADS * DIFF_DIM
    w_k12 = jnp.concatenate(
        [w_kv[:, :, :DIFF_DIM].reshape(d, nk), w_kv[:, :, DIFF_DIM:2 * DIFF_DIM].reshape(d, nk)],
        axis=1).astype(BF16)
    w_svT = w_kv[:, :, 2 * DIFF_DIM:].reshape(d, DIFF_HEADS * DIFF_V).T.astype(BF16)
    nq_cols = DIFF_HEADS * 2 * DIFF_DIM
    w_bq_T = b_w_in[0][:, :nq_cols].T.astype(BF16)
    w_bqm = b_w_in[0][:, nq_cols:].astype(BF16)
    q1T, q2T, k1, k2, vbT, omb = _b_proj(x, pos_row, _row(shared_kv_norm), _row(b_attn_norm_pre[0]),
                                         w_k12, w_svT, w_bq_T, w_bqm, mkT, mv, tm)
    lam_params = jnp.stack([b_lambda_q1[0], b_lambda_k1[0], b_lambda_q2[0], b_lambda_k2[0]]).astype(F32)
    ob = _diff_attn(q1T, q2T, k1, k2, vbT, lam_params, _row(b_subln[0]), lam_init, tq)
    n_main_b = DIFF_HEADS * DIFF_V
    w_ob = b_w_o[0].astype(BF16)
    x = _out_proj(ob, omb, x, w_ob[:n_main_b], w_ob[n_main_b:], _row(b_attn_norm_post[0]), tm)
    x = _conv_ffn(x, _row(ffn_norm_pre[1]), ffn_w_in[1].astype(BF16), ffn_conv_w[1],
                  _row(ffn_conv_b[1]), ffn_w_out[1].astype(BF16), _row(ffn_norm_post[1]), tm, fc)
    return x
```

```python
import functools
import math

import jax
import jax.numpy as jnp
from jax import lax
from jax.experimental import pallas as pl
from jax.experimental.pallas import tpu as pltpu

D_MODEL = 2048
ROPE_THETA = 10000.0
NORM_EPS = 1e-6
MLA_HEADS = 12
MLA_NOPE = 128
MLA_ROPE = 64
MLA_V = 128
MLA_DK = MLA_NOPE + MLA_ROPE
Q_LORA = 512
KV_LORA = 512
DIFF_HEADS = 6
DIFF_DIM = 128
DIFF_V = 2 * DIFF_DIM
MEM_HEADS = 4
MEM_DIM = 128
D_FF = 5632
CONV_WIDTH = 3
N_A_LAYERS = 1

LANES = 128
SUBLANES = 8
A_IN_PAD = Q_LORA + KV_LORA + LANES + MEM_HEADS * MEM_DIM

V7X_VMEM_BYTES = 64 * 1024 * 1024
VMEM_LIMIT = V7X_VMEM_BYTES - 8 * 1024 * 1024

NEG_BIG = -1e30
LOG2_E = math.log2(math.e)
BF16 = jnp.bfloat16
F32 = jnp.float32


def _tiles(seq):
    tm = min(512, seq)
    tq = min(1024, seq)
    fc = 512
    return tm, tq, fc


def _params(n_axes):
    return pltpu.CompilerParams(dimension_semantics=("arbitrary",) * n_axes,
                                vmem_limit_bytes=VMEM_LIMIT)


def _const_spec(shape):
    nd = len(shape)
    return pl.BlockSpec(shape, lambda *_: (0,) * nd, pipeline_mode=pl.Buffered(1))


def _rms(xf, gain):
    ms = jnp.mean(xf * xf, axis=-1, keepdims=True)
    return xf * lax.rsqrt(ms + NORM_EPS) * gain


def _dot(a, b):
    return jnp.dot(a, b, preferred_element_type=F32)


def _dot_nt(a, b):
    return lax.dot_general(a, b, (((1,), (1,)), ((), ())), preferred_element_type=F32)


def _rope_tables_rows(pos_row, half, dh):
    tokens = pos_row.shape[-1]
    j = lax.broadcasted_iota(jnp.int32, (half, tokens), 0).astype(F32)
    inv_freq = jnp.exp(j * (-math.log(ROPE_THETA) * 2.0 / dh))
    ang = pos_row.astype(F32) * inv_freq
    return jnp.cos(ang), jnp.sin(ang)


def _rope_rows(a, b, cos_r, sin_r):
    return a * cos_r - b * sin_r, b * cos_r + a * sin_r


def _mem_attention(qm, mkT_ref, mv_ref, out_ref):
    scale = MEM_DIM ** -0.5
    for h in range(MEM_HEADS):
        qh = (qm[:, h * MEM_DIM:(h + 1) * MEM_DIM] * scale).astype(BF16)
        s = _dot(qh, mkT_ref[0, 0, h])
        m = jnp.max(s, axis=-1, keepdims=True)
        p = jnp.exp(s - m)
        l = jnp.sum(p, axis=-1, keepdims=True)
        o = _dot(p.astype(BF16), mv_ref[0, 0, h])
        out_ref[0, :, h * MEM_DIM:(h + 1) * MEM_DIM] = (o / l).astype(out_ref.dtype)


def _mem_kv_kernel(mem_ref, g_ref, w_ref, kT_ref, v_ref):
    n = _rms(mem_ref[0], g_ref[0]).astype(BF16)
    kv = _dot(n, w_ref[0])
    for h in range(MEM_HEADS):
        base = h * 2 * MEM_DIM
        kT_ref[0, 0, h] = kv[:, base:base + MEM_DIM].T.astype(BF16)
        v_ref[0, 0, h] = kv[:, base + MEM_DIM:base + 2 * MEM_DIM].astype(BF16)


def _mem_kv(mem, gains, weights):
    b, m, d = mem.shape
    nl = gains.shape[0]
    return pl.pallas_call(
        _mem_kv_kernel,
        grid=(nl, b),
        in_specs=[
            pl.BlockSpec((1, m, d), lambda l, i: (i, 0, 0)),
            pl.BlockSpec((1, 1, d), lambda l, i: (l, 0, 0)),
            pl.BlockSpec((1, d, MEM_HEADS * 2 * MEM_DIM), lambda l, i: (l, 0, 0)),
        ],
        out_specs=[
            pl.BlockSpec((1, 1, MEM_HEADS, MEM_DIM, m), lambda l, i: (l, i, 0, 0, 0)),
            pl.BlockSpec((1, 1, MEM_HEADS, m, MEM_DIM), lambda l, i: (l, i, 0, 0, 0)),
        ],
        out_shape=[
            jax.ShapeDtypeStruct((nl, b, MEM_HEADS, MEM_DIM, m), BF16),
            jax.ShapeDtypeStruct((nl, b, MEM_HEADS, m, MEM_DIM), BF16),
        ],
        compiler_params=_params(2),
        name="mem_kv",
    )(mem, gains, weights)


def _a_proj_kernel(x_ref, pos_ref, g_ref, w_in_ref, qg_ref, w_uqT_ref, kvg_ref, w_k_ref, w_vT_ref,
                   mkT_ref, mv_ref, qT_ref, k_ref, vT_ref, om_ref):
    tm = x_ref.shape[1]
    h = _rms(x_ref[0], g_ref[...]).astype(BF16)
    proj = _dot(h, w_in_ref[...])
    ql = _rms(proj[:, :Q_LORA], qg_ref[...]).astype(BF16)
    kvl = _rms(proj[:, Q_LORA:Q_LORA + KV_LORA], kvg_ref[...]).astype(BF16)
    pe = proj[:, Q_LORA + KV_LORA:Q_LORA + KV_LORA + LANES]
    qm = proj[:, Q_LORA + KV_LORA + LANES:]

    half = MLA_ROPE // 2
    cos_r, sin_r = _rope_tables_rows(pos_ref[0], half, MLA_ROPE)

    peT = pe.T
    r1, r2 = _rope_rows(peT[0:half], peT[half:2 * half], cos_r, sin_r)
    kpe = jnp.concatenate([r1, r2, jnp.zeros((LANES - MLA_ROPE, tm), F32)], axis=0).T
    kpe = kpe[:, :MLA_ROPE].astype(BF16)

    scale = MLA_DK ** -0.5 * LOG2_E
    cos_q, sin_q = cos_r * scale, sin_r * scale
    qT = _dot_nt(w_uqT_ref[...], ql)
    kn = _dot(kvl, w_k_ref[...])
    vT = _dot_nt(w_vT_ref[...], kvl)
    for hd in range(MLA_HEADS):
        base = hd * MLA_DK
        qT_ref[0, hd, 0, 0:MLA_NOPE, :] = (qT[base:base + MLA_NOPE] * scale).astype(BF16)
        r1, r2 = _rope_rows(qT[base + MLA_NOPE:base + MLA_NOPE + half],
                            qT[base + MLA_NOPE + half:base + MLA_DK], cos_q, sin_q)
        qT_ref[0, hd, 0, MLA_NOPE:MLA_NOPE + half, :] = r1.astype(BF16)
        qT_ref[0, hd, 0, MLA_NOPE + half:MLA_DK, :] = r2.astype(BF16)
        k_ref[0, hd, :, 0:MLA_NOPE] = kn[:, hd * MLA_NOPE:(hd + 1) * MLA_NOPE].astype(BF16)
        k_ref[0, hd, :, MLA_NOPE:MLA_DK] = kpe
        vT_ref[0, hd, 0] = vT[hd * MLA_V:(hd + 1) * MLA_V].astype(BF16)

    _mem_attention(qm, mkT_ref, mv_ref, om_ref)


def _a_proj(x, pos_row, gain, w_in, qg, w_uqT, kvg, w_k, w_vT, mkT, mv, tm, tq):
    b, s, d = x.shape
    nt = s // tm
    r = tq // tm
    m = mkT.shape[-1]
    return pl.pallas_call(
        _a_proj_kernel,
        grid=(b, nt),
        in_specs=[
            pl.BlockSpec((1, tm, d), lambda i, t: (i, t, 0)),
            pl.BlockSpec((1, 1, tm), lambda i, t: (i, 0, t)),
            _const_spec(gain.shape),
            _const_spec(w_in.shape),
            _const_spec(qg.shape),
            _const_spec(w_uqT.shape),
            _const_spec(kvg.shape),
            _const_spec(w_k.shape),
            _const_spec(w_vT.shape),
            pl.BlockSpec((1, 1, MEM_HEADS, MEM_DIM, m), lambda i, t: (0, i, 0, 0, 0)),
            pl.BlockSpec((1, 1, MEM_HEADS, m, MEM_DIM), lambda i, t: (0, i, 0, 0, 0)),
        ],
        out_specs=[
            pl.BlockSpec((1, MLA_HEADS, 1, MLA_DK, tm), lambda i, t: (i, 0, t // r, 0, t % r)),
            pl.BlockSpec((1, MLA_HEADS, tm, MLA_DK), lambda i, t: (i, 0, t, 0)),
            pl.BlockSpec((1, MLA_HEADS, 1, MLA_V, tm), lambda i, t: (i, 0, t, 0, 0)),
            pl.BlockSpec((1, tm, MEM_HEADS * MEM_DIM), lambda i, t: (i, t, 0)),
        ],
        out_shape=[
            jax.ShapeDtypeStruct((b, MLA_HEADS, s // tq, MLA_DK, tq), BF16),
            jax.ShapeDtypeStruct((b, MLA_HEADS, s, MLA_DK), BF16),
            jax.ShapeDtypeStruct((b, MLA_HEADS, nt, MLA_V, tm), BF16),
            jax.ShapeDtypeStruct((b, s, MEM_HEADS * MEM_DIM), BF16),
        ],
        compiler_params=_params(2),
        name="a_proj",
    )(x, pos_row, gain, w_in, qg, w_uqT, kvg, w_k, w_vT, mkT, mv)


def _causal_masks_t(tk, tq):
    key = lax.broadcasted_iota(jnp.int32, (tk, tq), 0)
    qry = lax.broadcasted_iota(jnp.int32, (tk, tq), 1)
    return [key + d * tk <= qry for d in range(tq // tk)]


def _online_softmax_t(s, m_prev, l_prev, mask):
    if mask is not None:
        s = jnp.where(mask, s, NEG_BIG)
    m_new = jnp.maximum(m_prev, jnp.max(s, axis=0, keepdims=True))
    alpha = jnp.exp2(m_prev - m_new)
    p = jnp.exp2(s - m_new)
    l_new = alpha * l_prev + jnp.sum(p, axis=0, keepdims=True)
    return p.astype(BF16), m_new, l_new, alpha


KEY_BLOCK_UNROLL = 4


def _pipelined_key_blocks(qi, scores, consume, finish, init, masks):
    nd = len(masks)
    n_full = qi * nd
    scores(0, 0)

    def group(g, carry):
        j = KEY_BLOCK_UNROLL * g
        for u in range(KEY_BLOCK_UNROLL):
            scores(j + u + 1, (u + 1) % 2)
            carry = consume(u % 2, j + u, carry, None)
        return carry

    carry = lax.fori_loop(0, n_full // KEY_BLOCK_UNROLL, group, init)
    base = (n_full // KEY_BLOCK_UNROLL) * KEY_BLOCK_UNROLL

    for rem in range(0, KEY_BLOCK_UNROLL, math.gcd(nd, KEY_BLOCK_UNROLL)):
        @pl.when(n_full % KEY_BLOCK_UNROLL == rem)
        def _(rem=rem):
            blocks = [(base + u, None) for u in range(rem)] + [(n_full + d, masks[d]) for d in range(nd)]
            c = carry
            for idx, (j, mask) in enumerate(blocks):
                if idx + 1 < len(blocks):
                    scores(blocks[idx + 1][0], (idx + 1) % 2)
                c = consume(idx % 2, j, c, mask)
            finish(c)


def _mla_attn_kernel(qT_ref, k_ref, vT_ref, o_ref, sa_ref, sb_ref, acc_ref):
    tq = qT_ref.shape[4]
    tk = vT_ref.shape[4]
    qi = pl.program_id(2)
    qT = qT_ref[0, 0, 0]
    s_refs = (sa_ref, sb_ref)

    def scores(j, slot):
        off = pl.multiple_of(j * tk, tk)
        s_refs[slot][...] = _dot(k_ref[0, 0, pl.ds(off, tk), :], qT)

    def consume(slot, j, carry, mask):
        m_prev, l_prev = carry
        p, m_new, l_new, alpha = _online_softmax_t(s_refs[slot][...], m_prev, l_prev, mask)
        acc_ref[...] = alpha * acc_ref[...] + _dot(vT_ref[0, 0, j], p)
        return m_new, l_new

    def finish(carry):
        _, l = carry
        o_ref[0] = (acc_ref[...] / l).T.astype(o_ref.dtype)

    acc_ref[...] = jnp.zeros(acc_ref.shape, F32)
    init = (jnp.full((1, tq), NEG_BIG, F32), jnp.zeros((1, tq), F32))
    _pipelined_key_blocks(qi, scores, consume, finish, init, _causal_masks_t(tk, tq))


def _mla_attn(qT, k, vT):
    b, nh, nq, _, tq = qT.shape
    nkb, tk = vT.shape[2], vT.shape[4]
    s = nq * tq
    return pl.pallas_call(
        _mla_attn_kernel,
        grid=(b, nh, nq),
        in_specs=[
            pl.BlockSpec((1, 1, 1, MLA_DK, tq), lambda i, h, t: (i, h, t, 0, 0)),
            pl.BlockSpec((1, 1, s, MLA_DK), lambda i, h, t: (i, h, 0, 0)),
            pl.BlockSpec((1, 1, nkb, MLA_V, tk), lambda i, h, t: (i, h, 0, 0, 0)),
        ],
        out_specs=pl.BlockSpec((1, tq, MLA_V), lambda i, h, t: (i, t, h)),
        out_shape=jax.ShapeDtypeStruct((b, s, nh * MLA_V), BF16),
        scratch_shapes=[pltpu.VMEM((tk, tq), F32), pltpu.VMEM((tk, tq), F32),
                        pltpu.VMEM((MLA_V, tq), F32)],
        compiler_params=_params(3),
        name="mla_attn",
    )(qT, k, vT)


def _diff_attn_kernel(lam_init, q1T_ref, q2T_ref, k1_ref, k2_ref, vT_ref, lam_ref, sub_ref, o_ref,
                      s1a_ref, s1b_ref, s2a_ref, s2b_ref, acc1_ref, acc2_ref):
    tq = q1T_ref.shape[4]
    tk = vT_ref.shape[4]
    qi = pl.program_id(2)
    q1T = q1T_ref[0, 0, 0]
    q2T = q2T_ref[0, 0, 0]
    s1_refs = (s1a_ref, s1b_ref)
    s2_refs = (s2a_ref, s2b_ref)

    def scores(j, slot):
        off = pl.multiple_of(j * tk, tk)
        s1_refs[slot][...] = _dot(k1_ref[0, 0, pl.ds(off, tk), :], q1T)
        s2_refs[slot][...] = _dot(k2_ref[0, 0, pl.ds(off, tk), :], q2T)

    def consume(slot, j, carry, mask):
        m1, l1, m2, l2 = carry
        vT = vT_ref[0, 0, j]
        p1, m1, l1, a1 = _online_softmax_t(s1_refs[slot][...], m1, l1, mask)
        acc1_ref[...] = a1 * acc1_ref[...] + _dot(vT, p1)
        p2, m2, l2, a2 = _online_softmax_t(s2_refs[slot][...], m2, l2, mask)
        acc2_ref[...] = a2 * acc2_ref[...] + _dot(vT, p2)
        return m1, l1, m2, l2

    def finish(carry):
        _, l1, _, l2 = carry
        lp = lam_ref[...]
        lam = (jnp.exp(jnp.sum(lp[0:1] * lp[1:2], axis=-1, keepdims=True))
               - jnp.exp(jnp.sum(lp[2:3] * lp[3:4], axis=-1, keepdims=True)) + lam_init)
        oT = acc1_ref[...] / l1 - lam * (acc2_ref[...] / l2)
        o_ref[0] = (_rms(oT.T, sub_ref[...]) * (1.0 - lam_init)).astype(o_ref.dtype)

    acc1_ref[...] = jnp.zeros(acc1_ref.shape, F32)
    acc2_ref[...] = jnp.zeros(acc2_ref.shape, F32)
    m_init = jnp.full((1, tq), NEG_BIG, F32)
    l_init = jnp.zeros((1, tq), F32)
    _pipelined_key_blocks(qi, scores, consume, finish, (m_init, l_init, m_init, l_init),
                          _causal_masks_t(tk, tq))


def _diff_attn(q1T, q2T, k1, k2, vT, lam_params, subln, lam_init):
    b, nh, nq, _, tq = q1T.shape
    nkb, tk = vT.shape[2], vT.shape[4]
    s = nq * tq
    qT_spec = pl.BlockSpec((1, 1, 1, DIFF_DIM, tq), lambda i, h, t: (i, h, t, 0, 0))
    k_spec = pl.BlockSpec((1, 1, s, DIFF_DIM), lambda i, h, t: (i, h, 0, 0))
    return pl.pallas_call(
        functools.partial(_diff_attn_kernel, lam_init),
        grid=(b, nh, nq),
        in_specs=[
            qT_spec, qT_spec, k_spec, k_spec,
            pl.BlockSpec((1, 1, nkb, DIFF_V, tk), lambda i, h, t: (i, h, 0, 0, 0)),
            _const_spec(lam_params.shape),
            _const_spec(subln.shape),
        ],
        out_specs=pl.BlockSpec((1, tq, DIFF_V), lambda i, h, t: (i, t, h)),
        out_shape=jax.ShapeDtypeStruct((b, s, nh * DIFF_V), BF16),
        scratch_shapes=[pltpu.VMEM((tk, tq), F32)] * 4 + [pltpu.VMEM((DIFF_V, tq), F32)] * 2,
        compiler_params=_params(3),
        name="diff_attn",
    )(q1T, q2T, k1, k2, vT, lam_params, subln)


def _out_proj_kernel(o_ref, om_ref, x_ref, w1_ref, w2_ref, g_ref, out_ref):
    mix = _dot(o_ref[0], w1_ref[...]) + _dot(om_ref[0], w2_ref[...])
    out_ref[0] = x_ref[0] + _rms(mix, g_ref[...])


def _out_proj(o, om, x, w1, w2, gain, tm):
    b, s, d = x.shape
    return pl.pallas_call(
        _out_proj_kernel,
        grid=(b, s // tm),
        in_specs=[
            pl.BlockSpec((1, tm, o.shape[-1]), lambda i, t: (i, t, 0)),
            pl.BlockSpec((1, tm, om.shape[-1]), lambda i, t: (i, t, 0)),
            pl.BlockSpec((1, tm, d), lambda i, t: (i, t, 0)),
            _const_spec(w1.shape),
            _const_spec(w2.shape),
            _const_spec(gain.shape),
        ],
        out_specs=pl.BlockSpec((1, tm, d), lambda i, t: (i, t, 0)),
        out_shape=jax.ShapeDtypeStruct((b, s, d), F32),
        compiler_params=_params(2),
        name="out_proj",
    )(o, om, x, w1, w2, gain)


FFN_SUB = 256
FFN_ROWS = 256


def _gelu_tanh(g):
    c = math.sqrt(2.0 / math.pi)
    return 0.5 * g * (1.0 + jnp.tanh(c * (g + 0.044715 * (g * g * g))))


def _causal_conv(u_ref, r, cw, cb):
    base = SUBLANES + r
    u0 = u_ref[base:base + FFN_ROWS, :]
    u1 = u_ref[base - 1:base - 1 + FFN_ROWS, :]
    u2 = u_ref[base - 2:base - 2 + FFN_ROWS, :]
    return cb + cw[0:1] * u2 + cw[1:2] * u1 + cw[2:3] * u0


def _conv_ffn_kernel(x_ref, pre_ref, wg_ref, wv_ref, cwg_ref, cwv_ref, cbg_ref, cbv_ref, wo_ref,
                     post_ref, out_ref, h_ref, acc_ref, tail_ref, u_ref):
    t = pl.program_id(1)
    c = pl.program_id(2)
    nc = pl.num_programs(2)
    tm = x_ref.shape[1]
    fc = wg_ref.shape[1]

    @pl.when(c == 0)
    def _():
        h_ref[...] = _rms(x_ref[0], pre_ref[...]).astype(BF16)
        acc_ref[...] = jnp.zeros(acc_ref.shape, F32)

    @pl.when(t == 0)
    def _():
        tail_ref[c] = jnp.zeros(tail_ref.shape[1:], F32)

    subs = list(range(0, fc, FFN_SUB))
    for si, lo in enumerate(subs):
        u_ref[si, 0, 0:SUBLANES, :] = tail_ref[c, 0, :, lo:lo + FFN_SUB]
        u_ref[si, 1, 0:SUBLANES, :] = tail_ref[c, 1, :, lo:lo + FFN_SUB]

    groups = [(si, r) for si in range(len(subs)) for r in range(0, tm, FFN_ROWS)]

    def up(si, r):
        lo = subs[si]
        hr = h_ref[r:r + FFN_ROWS, :]
        rows = slice(SUBLANES + r, SUBLANES + r + FFN_ROWS)
        u_ref[si, 0, rows, :] = _dot(hr, wg_ref[:, lo:lo + FFN_SUB])
        u_ref[si, 1, rows, :] = _dot(hr, wv_ref[:, lo:lo + FFN_SUB])

    up(*groups[0])
    for i, (si, r) in enumerate(groups):
        lo, hi = subs[si], subs[si] + FFN_SUB
        if i + 1 < len(groups):
            up(*groups[i + 1])
        cg = _causal_conv(u_ref.at[si, 0], r, cwg_ref[:, lo:hi], cbg_ref[:, lo:hi])
        cv = _causal_conv(u_ref.at[si, 1], r, cwv_ref[:, lo:hi], cbv_ref[:, lo:hi])
        y = (_gelu_tanh(cg) * cv).astype(BF16)
        acc_ref[r:r + FFN_ROWS, :] += _dot(y, wo_ref[lo:hi, :])

    for si, lo in enumerate(subs):
        tail_ref[c, 0, :, lo:lo + FFN_SUB] = u_ref[si, 0, tm:tm + SUBLANES, :]
        tail_ref[c, 1, :, lo:lo + FFN_SUB] = u_ref[si, 1, tm:tm + SUBLANES, :]

    @pl.when(c == nc - 1)
    def _():
        out_ref[0] = x_ref[0] + _rms(acc_ref[...], post_ref[...])


def _conv_ffn(x, pre, w_in, conv_w, conv_b, w_out, post, tm, fc):
    b, s, d = x.shape
    nc = D_FF // fc
    return pl.pallas_call(
        _conv_ffn_kernel,
        grid=(b, s // tm, nc),
        in_specs=[
            pl.BlockSpec((1, tm, d), lambda i, t, c: (i, t, 0)),
            _const_spec(pre.shape),
            pl.BlockSpec((d, fc), lambda i, t, c: (0, c)),
            pl.BlockSpec((d, fc), lambda i, t, c: (0, c + nc)),
            pl.BlockSpec((CONV_WIDTH, fc), lambda i, t, c: (0, c)),
            pl.BlockSpec((CONV_WIDTH, fc), lambda i, t, c: (0, c + nc)),
            pl.BlockSpec((1, fc), lambda i, t, c: (0, c)),
            pl.BlockSpec((1, fc), lambda i, t, c: (0, c + nc)),
            pl.BlockSpec((fc, d), lambda i, t, c: (c, 0)),
            _const_spec(post.shape),
        ],
        out_specs=pl.BlockSpec((1, tm, d), lambda i, t, c: (i, t, 0)),
        out_shape=jax.ShapeDtypeStruct((b, s, d), F32),
        scratch_shapes=[
            pltpu.VMEM((tm, d), BF16),
            pltpu.VMEM((tm, d), F32),
            pltpu.VMEM((nc, 2, SUBLANES, fc), F32),
            pltpu.VMEM((fc // FFN_SUB, 2, SUBLANES + tm, FFN_SUB), F32),
        ],
        compiler_params=_params(3),
        name="conv_ffn",
    )(x, pre, w_in, w_in, conv_w, conv_w, conv_b, conv_b, w_out, post)


def _b_proj_kernel(x_ref, pos_ref, gs_ref, gb_ref, w_k_ref, w_vT_ref, w_qT_ref, w_qm_ref,
                   mkT_ref, mv_ref, q1T_ref, q2T_ref, k1_ref, k2_ref, vT_ref, om_ref):
    xf = x_ref[0]
    n = xf * lax.rsqrt(jnp.mean(xf * xf, axis=-1, keepdims=True) + NORM_EPS)
    hs = (n * gs_ref[...]).astype(BF16)
    hb = (n * gb_ref[...]).astype(BF16)

    half = DIFF_DIM // 2
    cos_r, sin_r = _rope_tables_rows(pos_ref[0], half, DIFF_DIM)

    tab_c = jnp.concatenate([cos_r, cos_r], axis=0).T
    tab_s = jnp.concatenate([-sin_r, sin_r], axis=0).T
    kk = _dot(hs, w_k_ref[...])
    vT = _dot_nt(w_vT_ref[...], hs)
    nk = DIFF_HEADS * DIFF_DIM
    for hd in range(DIFF_HEADS):
        for which, ref in ((0, k1_ref), (1, k2_ref)):
            base = which * nk + hd * DIFF_DIM
            xr = kk[:, base:base + DIFF_DIM]
            ref[0, hd] = (xr * tab_c + pltpu.roll(xr, half, 1) * tab_s).astype(BF16)
        vT_ref[0, hd, 0] = vT[hd * DIFF_V:(hd + 1) * DIFF_V].astype(BF16)

    scale = DIFF_DIM ** -0.5 * LOG2_E
    cos_q, sin_q = cos_r * scale, sin_r * scale
    qT = _dot_nt(w_qT_ref[...], hb)
    for hd in range(DIFF_HEADS):
        for which, ref in ((0, q1T_ref), (1, q2T_ref)):
            base = hd * 2 * DIFF_DIM + which * DIFF_DIM
            r1, r2 = _rope_rows(qT[base:base + half], qT[base + half:base + DIFF_DIM], cos_q, sin_q)
            ref[0, hd, 0] = jnp.concatenate([r1, r2], axis=0).astype(BF16)

    _mem_attention(_dot(hb, w_qm_ref[...]), mkT_ref, mv_ref, om_ref)


def _b_proj(x, pos_row, gs, gb, w_k, w_vT, w_qT, w_qm, mkT, mv, tm, tq):
    b, s, d = x.shape
    nt = s // tm
    r = tq // tm
    m = mkT.shape[-1]
    qT_shape = jax.ShapeDtypeStruct((b, DIFF_HEADS, s // tq, DIFF_DIM, tq), BF16)
    k_shape = jax.ShapeDtypeStruct((b, DIFF_HEADS, s, DIFF_DIM), BF16)
    qT_spec = pl.BlockSpec((1, DIFF_HEADS, 1, DIFF_DIM, tm), lambda i, t: (i, 0, t // r, 0, t % r))
    k_spec = pl.BlockSpec((1, DIFF_HEADS, tm, DIFF_DIM), lambda i, t: (i, 0, t, 0))
    return pl.pallas_call(
        _b_proj_kernel,
        grid=(b, nt),
        in_specs=[
            pl.BlockSpec((1, tm, d), lambda i, t: (i, t, 0)),
            pl.BlockSpec((1, 1, tm), lambda i, t: (i, 0, t)),
            _const_spec(gs.shape),
            _const_spec(gb.shape),
            _const_spec(w_k.shape),
            _const_spec(w_vT.shape),
            _const_spec(w_qT.shape),
            _const_spec(w_qm.shape),
            pl.BlockSpec((1, 1, MEM_HEADS, MEM_DIM, m), lambda i, t: (1, i, 0, 0, 0)),
            pl.BlockSpec((1, 1, MEM_HEADS, m, MEM_DIM), lambda i, t: (1, i, 0, 0, 0)),
        ],
        out_specs=[
            qT_spec, qT_spec, k_spec, k_spec,
            pl.BlockSpec((1, DIFF_HEADS, 1, DIFF_V, tm), lambda i, t: (i, 0, t, 0, 0)),
            pl.BlockSpec((1, tm, MEM_HEADS * MEM_DIM), lambda i, t: (i, t, 0)),
        ],
        out_shape=[
            qT_shape, qT_shape, k_shape, k_shape,
            jax.ShapeDtypeStruct((b, DIFF_HEADS, nt, DIFF_V, tm), BF16),
            jax.ShapeDtypeStruct((b, s, MEM_HEADS * MEM_DIM), BF16),
        ],
        compiler_params=_params(2),
        name="b_proj",
    )(x, pos_row, gs, gb, w_k, w_vT, w_qT, w_qm, mkT, mv)


def _row(v):
    return v.reshape(1, -1).astype(F32)


def kernel(x, mem, positions, a_attn_norm_pre, a_w_in, a_q_norm, a_w_uq, a_kv_norm, a_w_ukv, a_mem_norm, a_w_mem_kv, a_w_o, a_attn_norm_post, shared_kv_norm, shared_w_kv, b_attn_norm_pre, b_w_in, b_lambda_q1, b_lambda_k1, b_lambda_q2, b_lambda_k2, b_subln, b_mem_norm, b_w_mem_kv, b_w_o, b_attn_norm_post, ffn_norm_pre, ffn_w_in, ffn_conv_w, ffn_conv_b, ffn_w_out, ffn_norm_post):
    b, s, d = x.shape
    assert d == D_MODEL and a_attn_norm_pre.shape[0] == 1 and b_attn_norm_pre.shape[0] == 1
    tm, tq, fc = _tiles(s)
    assert s % tq == 0 and tq % tm == 0 and D_FF % fc == 0
    pos_row = positions.reshape(b, 1, s)

    mem_gains = jnp.stack([a_mem_norm[0], b_mem_norm[0]]).reshape(2, 1, d)
    mem_w = jnp.stack([a_w_mem_kv[0], b_w_mem_kv[0]]).astype(BF16)
    mkT, mv = _mem_kv(mem, mem_gains, mem_w)

    w_in = a_w_in[0]
    i2 = Q_LORA + KV_LORA + MLA_ROPE
    w_in_p = jnp.concatenate(
        [w_in[:, :i2], jnp.zeros((d, LANES - MLA_ROPE), F32), w_in[:, i2:]], axis=1).astype(BF16)
    w_uqT = a_w_uq[0].T.astype(BF16)
    w_ukv = a_w_ukv[0].reshape(KV_LORA, MLA_HEADS, MLA_NOPE + MLA_V)
    w_k = w_ukv[:, :, :MLA_NOPE].reshape(KV_LORA, MLA_HEADS * MLA_NOPE).astype(BF16)
    w_vT = w_ukv[:, :, MLA_NOPE:].reshape(KV_LORA, MLA_HEADS * MLA_V).T.astype(BF16)
    qT, k, vT, om = _a_proj(x, pos_row, _row(a_attn_norm_pre[0]), w_in_p, _row(a_q_norm[0]), w_uqT,
                            _row(a_kv_norm[0]), w_k, w_vT, mkT, mv, tm, tq)
    o = _mla_attn(qT, k, vT)
    n_main = MLA_HEADS * MLA_V
    w_o = a_w_o[0].astype(BF16)
    x = _out_proj(o, om, x, w_o[:n_main], w_o[n_main:], _row(a_attn_norm_post[0]), tm)
    x = _conv_ffn(x, _row(ffn_norm_pre[0]), ffn_w_in[0].astype(BF16), ffn_conv_w[0],
                  _row(ffn_conv_b[0]), ffn_w_out[0].astype(BF16), _row(ffn_norm_post[0]), tm, fc)

    layer = N_A_LAYERS
    lam_init = 0.8 - 0.6 * math.exp(-0.3 * layer)
    w_kv = shared_w_kv.reshape(d, DIFF_HEADS, 2 * DIFF_DIM + DIFF_V)
    nk = DIFF_HEADS * DIFF_DIM
    w_k12 = jnp.concatenate(
        [w_kv[:, :, :DIFF_DIM].reshape(d, nk), w_kv[:, :, DIFF_DIM:2 * DIFF_DIM].reshape(d, nk)],
        axis=1).astype(BF16)
    w_svT = w_kv[:, :, 2 * DIFF_DIM:].reshape(d, DIFF_HEADS * DIFF_V).T.astype(BF16)
    nq_cols = DIFF_HEADS * 2 * DIFF_DIM
    w_bq_T = b_w_in[0][:, :nq_cols].T.astype(BF16)
    w_bqm = b_w_in[0][:, nq_cols:].astype(BF16)
    q1T, q2T, k1, k2, vbT, omb = _b_proj(x, pos_row, _row(shared_kv_norm), _row(b_attn_norm_pre[0]),
                                         w_k12, w_svT, w_bq_T, w_bqm, mkT, mv, tm, tq)
    lam_params = jnp.stack([b_lambda_q1[0], b_lambda_k1[0], b_lambda_q2[0], b_lambda_k2[0]]).astype(F32)
    ob = _diff_attn(q1T, q2T, k1, k2, vbT, lam_params, _row(b_subln[0]), lam_init)
    n_main_b = DIFF_HEADS * DIFF_V
    w_ob = b_w_o[0].astype(BF16)
    x = _out_proj(ob, omb, x, w_ob[:n_main_b], w_ob[n_main_b:], _row(b_attn_norm_post[0]), tm)
    x = _conv_ffn(x, _row(ffn_norm_pre[1]), ffn_w_in[1].astype(BF16), ffn_conv_w[1],
                  _row(ffn_conv_b[1]), ffn_w_out[1].astype(BF16), _row(ffn_norm_post[1]), tm, fc)
    return x
```

```python
import functools
import math

import jax
import jax.numpy as jnp
from jax import lax
from jax.experimental import pallas as pl
from jax.experimental.pallas import tpu as pltpu

D_MODEL = 2048
ROPE_THETA = 10000.0
NORM_EPS = 1e-6
MLA_HEADS = 12
MLA_NOPE = 128
MLA_ROPE = 64
MLA_V = 128
MLA_DK = MLA_NOPE + MLA_ROPE
Q_LORA = 512
KV_LORA = 512
DIFF_HEADS = 6
DIFF_DIM = 128
DIFF_V = 2 * DIFF_DIM
MEM_HEADS = 4
MEM_DIM = 128
D_FF = 5632
CONV_WIDTH = 3
N_A_LAYERS = 1

LANES = 128
SUBLANES = 8
SUM_ROWS = 16
A_IN_PAD = Q_LORA + KV_LORA + LANES + MEM_HEADS * MEM_DIM

V7X_VMEM_BYTES = 64 * 1024 * 1024
VMEM_LIMIT = V7X_VMEM_BYTES - 8 * 1024 * 1024

NEG_BIG = -1e30
LOG2_E = math.log2(math.e)
BF16 = jnp.bfloat16
F32 = jnp.float32


def _tiles(seq):
    tm = min(512, seq)
    tq = min(1024, seq)
    fc = 512
    return tm, tq, fc


def _params(n_axes):
    return pltpu.CompilerParams(dimension_semantics=("arbitrary",) * n_axes,
                                vmem_limit_bytes=VMEM_LIMIT)


def _const_spec(shape):
    nd = len(shape)
    return pl.BlockSpec(shape, lambda *_: (0,) * nd, pipeline_mode=pl.Buffered(1))


def _rms(xf, gain):
    ms = jnp.mean(xf * xf, axis=-1, keepdims=True)
    return xf * lax.rsqrt(ms + NORM_EPS) * gain


def _dot(a, b):
    return jnp.dot(a, b, preferred_element_type=F32)


def _dot_nt(a, b):
    return lax.dot_general(a, b, (((1,), (1,)), ((), ())), preferred_element_type=F32)


def _rope_tables_rows(pos_row, half, dh):
    tokens = pos_row.shape[-1]
    j = lax.broadcasted_iota(jnp.int32, (half, tokens), 0).astype(F32)
    inv_freq = jnp.exp(j * (-math.log(ROPE_THETA) * 2.0 / dh))
    ang = pos_row.astype(F32) * inv_freq
    return jnp.cos(ang), jnp.sin(ang)


def _rope_rows(a, b, cos_r, sin_r):
    return a * cos_r - b * sin_r, b * cos_r + a * sin_r


def _mem_attention(qm, mkT_ref, mv_ref, out_ref):
    scale = MEM_DIM ** -0.5
    for h in range(MEM_HEADS):
        qh = (qm[:, h * MEM_DIM:(h + 1) * MEM_DIM] * scale).astype(BF16)
        s = _dot(qh, mkT_ref[0, 0, h])
        m = jnp.max(s, axis=-1, keepdims=True)
        p = jnp.exp(s - m)
        l = jnp.sum(p, axis=-1, keepdims=True)
        o = _dot(p.astype(BF16), mv_ref[0, 0, h])
        out_ref[0, :, h * MEM_DIM:(h + 1) * MEM_DIM] = (o / l).astype(out_ref.dtype)


def _mem_kv_kernel(mem_ref, g_ref, w_ref, kT_ref, v_ref):
    n = _rms(mem_ref[0], g_ref[0]).astype(BF16)
    kv = _dot(n, w_ref[0])
    for h in range(MEM_HEADS):
        base = h * 2 * MEM_DIM
        kT_ref[0, 0, h] = kv[:, base:base + MEM_DIM].T.astype(BF16)
        v_ref[0, 0, h] = kv[:, base + MEM_DIM:base + 2 * MEM_DIM].astype(BF16)


def _mem_kv(mem, gains, weights):
    b, m, d = mem.shape
    nl = gains.shape[0]
    return pl.pallas_call(
        _mem_kv_kernel,
        grid=(nl, b),
        in_specs=[
            pl.BlockSpec((1, m, d), lambda l, i: (i, 0, 0)),
            pl.BlockSpec((1, 1, d), lambda l, i: (l, 0, 0)),
            pl.BlockSpec((1, d, MEM_HEADS * 2 * MEM_DIM), lambda l, i: (l, 0, 0)),
        ],
        out_specs=[
            pl.BlockSpec((1, 1, MEM_HEADS, MEM_DIM, m), lambda l, i: (l, i, 0, 0, 0)),
            pl.BlockSpec((1, 1, MEM_HEADS, m, MEM_DIM), lambda l, i: (l, i, 0, 0, 0)),
        ],
        out_shape=[
            jax.ShapeDtypeStruct((nl, b, MEM_HEADS, MEM_DIM, m), BF16),
            jax.ShapeDtypeStruct((nl, b, MEM_HEADS, m, MEM_DIM), BF16),
        ],
        compiler_params=_params(2),
        name="mem_kv",
    )(mem, gains, weights)


def _a_proj_kernel(x_ref, pos_ref, g_ref, w_in_ref, qg_ref, w_uqT_ref, kvg_ref, w_k_ref, w_vT_ref,
                   mkT_ref, mv_ref, qT_ref, k_ref, vT_ref, om_ref):
    tm = x_ref.shape[1]
    h = _rms(x_ref[0], g_ref[...]).astype(BF16)
    proj = _dot(h, w_in_ref[...])
    ql = _rms(proj[:, :Q_LORA], qg_ref[...]).astype(BF16)
    kvl = _rms(proj[:, Q_LORA:Q_LORA + KV_LORA], kvg_ref[...]).astype(BF16)
    pe = proj[:, Q_LORA + KV_LORA:Q_LORA + KV_LORA + LANES]
    qm = proj[:, Q_LORA + KV_LORA + LANES:]

    half = MLA_ROPE // 2
    cos_r, sin_r = _rope_tables_rows(pos_ref[0], half, MLA_ROPE)

    peT = pe.T
    r1, r2 = _rope_rows(peT[0:half], peT[half:2 * half], cos_r, sin_r)
    kpe = jnp.concatenate([r1, r2, jnp.zeros((LANES - MLA_ROPE, tm), F32)], axis=0).T
    kpe = kpe[:, :MLA_ROPE].astype(BF16)

    scale = MLA_DK ** -0.5 * LOG2_E
    cos_q, sin_q = cos_r * scale, sin_r * scale
    qT = _dot_nt(w_uqT_ref[...], ql)
    kn = _dot(kvl, w_k_ref[...])
    vT = _dot_nt(w_vT_ref[...], kvl)
    for hd in range(MLA_HEADS):
        base = hd * MLA_DK
        qT_ref[0, hd, 0, 0:MLA_NOPE, :] = (qT[base:base + MLA_NOPE] * scale).astype(BF16)
        r1, r2 = _rope_rows(qT[base + MLA_NOPE:base + MLA_NOPE + half],
                            qT[base + MLA_NOPE + half:base + MLA_DK], cos_q, sin_q)
        qT_ref[0, hd, 0, MLA_NOPE:MLA_NOPE + half, :] = r1.astype(BF16)
        qT_ref[0, hd, 0, MLA_NOPE + half:MLA_DK, :] = r2.astype(BF16)
        k_ref[0, hd, :, 0:MLA_NOPE] = kn[:, hd * MLA_NOPE:(hd + 1) * MLA_NOPE].astype(BF16)
        k_ref[0, hd, :, MLA_NOPE:MLA_DK] = kpe
        vT_ref[0, hd, 0, 0:MLA_V, :] = vT[hd * MLA_V:(hd + 1) * MLA_V].astype(BF16)
        vT_ref[0, hd, 0, MLA_V:MLA_V + SUM_ROWS, :] = jnp.ones((SUM_ROWS, tm), BF16)

    _mem_attention(qm, mkT_ref, mv_ref, om_ref)


def _a_proj(x, pos_row, gain, w_in, qg, w_uqT, kvg, w_k, w_vT, mkT, mv, tm, tq):
    b, s, d = x.shape
    nt = s // tm
    r = tq // tm
    m = mkT.shape[-1]
    return pl.pallas_call(
        _a_proj_kernel,
        grid=(b, nt),
        in_specs=[
            pl.BlockSpec((1, tm, d), lambda i, t: (i, t, 0)),
            pl.BlockSpec((1, 1, tm), lambda i, t: (i, 0, t)),
            _const_spec(gain.shape),
            _const_spec(w_in.shape),
            _const_spec(qg.shape),
            _const_spec(w_uqT.shape),
            _const_spec(kvg.shape),
            _const_spec(w_k.shape),
            _const_spec(w_vT.shape),
            pl.BlockSpec((1, 1, MEM_HEADS, MEM_DIM, m), lambda i, t: (0, i, 0, 0, 0)),
            pl.BlockSpec((1, 1, MEM_HEADS, m, MEM_DIM), lambda i, t: (0, i, 0, 0, 0)),
        ],
        out_specs=[
            pl.BlockSpec((1, MLA_HEADS, 1, MLA_DK, tm), lambda i, t: (i, 0, t // r, 0, t % r)),
            pl.BlockSpec((1, MLA_HEADS, tm, MLA_DK), lambda i, t: (i, 0, t, 0)),
            pl.BlockSpec((1, MLA_HEADS, 1, MLA_V + SUM_ROWS, tm), lambda i, t: (i, 0, t, 0, 0)),
            pl.BlockSpec((1, tm, MEM_HEADS * MEM_DIM), lambda i, t: (i, t, 0)),
        ],
        out_shape=[
            jax.ShapeDtypeStruct((b, MLA_HEADS, s // tq, MLA_DK, tq), BF16),
            jax.ShapeDtypeStruct((b, MLA_HEADS, s, MLA_DK), BF16),
            jax.ShapeDtypeStruct((b, MLA_HEADS, nt, MLA_V + SUM_ROWS, tm), BF16),
            jax.ShapeDtypeStruct((b, s, MEM_HEADS * MEM_DIM), BF16),
        ],
        compiler_params=_params(2),
        name="a_proj",
    )(x, pos_row, gain, w_in, qg, w_uqT, kvg, w_k, w_vT, mkT, mv)


def _online_softmax_t(s, m_prev, masked):
    if masked:
        key = lax.broadcasted_iota(jnp.int32, s.shape, 0)
        qry = lax.broadcasted_iota(jnp.int32, s.shape, 1)
        s = jnp.where(key <= qry, s, NEG_BIG)
    m_new = jnp.maximum(m_prev, jnp.max(s, axis=0, keepdims=True))
    alpha = jnp.exp2(m_prev - m_new)
    p = jnp.exp2(s - m_new)
    return p.astype(BF16), m_new, alpha


def _fold_key_block(s_ref, acc_ref, vT, m_prev, q_lo):
    if q_lo is None:
        p, m_new, alpha = _online_softmax_t(s_ref[...], m_prev, False)
        acc_ref[...] = alpha * acc_ref[...] + _dot(vT, p)
        return m_new
    p, m_part, alpha = _online_softmax_t(s_ref[:, q_lo:], m_prev[:, q_lo:], True)
    acc_ref[:, q_lo:] = alpha * acc_ref[:, q_lo:] + _dot(vT, p)
    return m_part if q_lo == 0 else jnp.concatenate([m_prev[:, :q_lo], m_part], axis=1)


KEY_BLOCK_UNROLL = 4


def _pipelined_key_blocks(qi, nd, tk, scores, consume, finish, init):
    n_full = qi * nd
    scores(0, 0, None)

    def group(g, carry):
        j = KEY_BLOCK_UNROLL * g
        for u in range(KEY_BLOCK_UNROLL):
            scores(j + u + 1, (u + 1) % 2, None)
            carry = consume(u % 2, j + u, carry, None)
        return carry

    carry = lax.fori_loop(0, n_full // KEY_BLOCK_UNROLL, group, init)
    base = (n_full // KEY_BLOCK_UNROLL) * KEY_BLOCK_UNROLL

    for rem in range(0, KEY_BLOCK_UNROLL, math.gcd(nd, KEY_BLOCK_UNROLL)):
        @pl.when(n_full % KEY_BLOCK_UNROLL == rem)
        def _(rem=rem):
            blocks = [(base + u, None) for u in range(rem)] + [(n_full + d, d * tk) for d in range(nd)]
            c = carry
            for idx, (j, q_lo) in enumerate(blocks):
                if idx + 1 < len(blocks):
                    scores(blocks[idx + 1][0], (idx + 1) % 2, blocks[idx + 1][1])
                c = consume(idx % 2, j, c, q_lo)
            finish(c)


def _mla_attn_kernel(qT_ref, k_ref, vT_ref, o_ref, sa_ref, sb_ref, acc_ref):
    tq = qT_ref.shape[4]
    tk = vT_ref.shape[4]
    qi = pl.program_id(2)
    qT = qT_ref[0, 0, 0]
    s_refs = (sa_ref, sb_ref)

    def scores(j, slot, q_lo):
        off = pl.multiple_of(j * tk, tk)
        lo = q_lo or 0
        s_refs[slot][:, lo:] = _dot(k_ref[0, 0, pl.ds(off, tk), :], qT[:, lo:])

    def consume(slot, j, carry, q_lo):
        return _fold_key_block(s_refs[slot], acc_ref, vT_ref[0, 0, j], carry, q_lo)

    def finish(carry):
        l = acc_ref[MLA_V:MLA_V + 1, :]
        o_ref[0] = (acc_ref[0:MLA_V, :] / l).T.astype(o_ref.dtype)

    acc_ref[...] = jnp.zeros(acc_ref.shape, F32)
    init = jnp.full((1, tq), NEG_BIG, F32)
    _pipelined_key_blocks(qi, tq // tk, tk, scores, consume, finish, init)


def _mla_attn(qT, k, vT):
    b, nh, nq, _, tq = qT.shape
    nkb, tk = vT.shape[2], vT.shape[4]
    s = nq * tq
    return pl.pallas_call(
        _mla_attn_kernel,
        grid=(b, nh, nq),
        in_specs=[
            pl.BlockSpec((1, 1, 1, MLA_DK, tq), lambda i, h, t: (i, h, t, 0, 0)),
            pl.BlockSpec((1, 1, s, MLA_DK), lambda i, h, t: (i, h, 0, 0)),
            pl.BlockSpec((1, 1, nkb, MLA_V + SUM_ROWS, tk), lambda i, h, t: (i, h, 0, 0, 0)),
        ],
        out_specs=pl.BlockSpec((1, tq, MLA_V), lambda i, h, t: (i, t, h)),
        out_shape=jax.ShapeDtypeStruct((b, s, nh * MLA_V), BF16),
        scratch_shapes=[pltpu.VMEM((tk, tq), F32), pltpu.VMEM((tk, tq), F32),
                        pltpu.VMEM((MLA_V + SUM_ROWS, tq), F32)],
        compiler_params=_params(3),
        name="mla_attn",
    )(qT, k, vT)


def _diff_attn_kernel(lam_init, q1T_ref, q2T_ref, k1_ref, k2_ref, vT_ref, lam_ref, sub_ref, o_ref,
                      s1a_ref, s1b_ref, s2a_ref, s2b_ref, acc1_ref, acc2_ref):
    tq = q1T_ref.shape[4]
    tk = vT_ref.shape[4]
    qi = pl.program_id(2)
    q1T = q1T_ref[0, 0, 0]
    q2T = q2T_ref[0, 0, 0]
    s1_refs = (s1a_ref, s1b_ref)
    s2_refs = (s2a_ref, s2b_ref)

    def scores(j, slot, q_lo):
        off = pl.multiple_of(j * tk, tk)
        lo = q_lo or 0
        s1_refs[slot][:, lo:] = _dot(k1_ref[0, 0, pl.ds(off, tk), :], q1T[:, lo:])
        s2_refs[slot][:, lo:] = _dot(k2_ref[0, 0, pl.ds(off, tk), :], q2T[:, lo:])

    def consume(slot, j, carry, q_lo):
        vT = vT_ref[0, 0, j]
        m1 = _fold_key_block(s1_refs[slot], acc1_ref, vT, carry[0], q_lo)
        m2 = _fold_key_block(s2_refs[slot], acc2_ref, vT, carry[1], q_lo)
        return m1, m2

    def finish(carry):
        l1 = acc1_ref[DIFF_V:DIFF_V + 1, :]
        l2 = acc2_ref[DIFF_V:DIFF_V + 1, :]
        lp = lam_ref[...]
        lam = (jnp.exp(jnp.sum(lp[0:1] * lp[1:2], axis=-1, keepdims=True))
               - jnp.exp(jnp.sum(lp[2:3] * lp[3:4], axis=-1, keepdims=True)) + lam_init)
        oT = acc1_ref[0:DIFF_V, :] / l1 - lam * (acc2_ref[0:DIFF_V, :] / l2)
        o_ref[0] = (_rms(oT.T, sub_ref[...]) * (1.0 - lam_init)).astype(o_ref.dtype)

    acc1_ref[...] = jnp.zeros(acc1_ref.shape, F32)
    acc2_ref[...] = jnp.zeros(acc2_ref.shape, F32)
    m_init = jnp.full((1, tq), NEG_BIG, F32)
    _pipelined_key_blocks(qi, tq // tk, tk, scores, consume, finish, (m_init, m_init))


def _diff_attn(q1T, q2T, k1, k2, vT, lam_params, subln, lam_init):
    b, nh, nq, _, tq = q1T.shape
    nkb, tk = vT.shape[2], vT.shape[4]
    s = nq * tq
    qT_spec = pl.BlockSpec((1, 1, 1, DIFF_DIM, tq), lambda i, h, t: (i, h, t, 0, 0))
    k_spec = pl.BlockSpec((1, 1, s, DIFF_DIM), lambda i, h, t: (i, h, 0, 0))
    return pl.pallas_call(
        functools.partial(_diff_attn_kernel, lam_init),
        grid=(b, nh, nq),
        in_specs=[
            qT_spec, qT_spec, k_spec, k_spec,
            pl.BlockSpec((1, 1, nkb, DIFF_V + SUM_ROWS, tk), lambda i, h, t: (i, h, 0, 0, 0)),
            _const_spec(lam_params.shape),
            _const_spec(subln.shape),
        ],
        out_specs=pl.BlockSpec((1, tq, DIFF_V), lambda i, h, t: (i, t, h)),
        out_shape=jax.ShapeDtypeStruct((b, s, nh * DIFF_V), BF16),
        scratch_shapes=[pltpu.VMEM((tk, tq), F32)] * 4 + [pltpu.VMEM((DIFF_V + SUM_ROWS, tq), F32)] * 2,
        compiler_params=_params(3),
        name="diff_attn",
    )(q1T, q2T, k1, k2, vT, lam_params, subln)


def _out_proj_kernel(o_ref, om_ref, x_ref, w1_ref, w2_ref, g_ref, out_ref):
    mix = _dot(o_ref[0], w1_ref[...]) + _dot(om_ref[0], w2_ref[...])
    out_ref[0] = x_ref[0] + _rms(mix, g_ref[...])


def _out_proj(o, om, x, w1, w2, gain, tm):
    b, s, d = x.shape
    return pl.pallas_call(
        _out_proj_kernel,
        grid=(b, s // tm),
        in_specs=[
            pl.BlockSpec((1, tm, o.shape[-1]), lambda i, t: (i, t, 0)),
            pl.BlockSpec((1, tm, om.shape[-1]), lambda i, t: (i, t, 0)),
            pl.BlockSpec((1, tm, d), lambda i, t: (i, t, 0)),
            _const_spec(w1.shape),
            _const_spec(w2.shape),
            _const_spec(gain.shape),
        ],
        out_specs=pl.BlockSpec((1, tm, d), lambda i, t: (i, t, 0)),
        out_shape=jax.ShapeDtypeStruct((b, s, d), F32),
        compiler_params=_params(2),
        name="out_proj",
    )(o, om, x, w1, w2, gain)


FFN_SUB = 256
FFN_ROWS = 256
NORM_ROWS = 32


def _gelu_tanh(g):
    c = math.sqrt(2.0 / math.pi)
    return 0.5 * g * (1.0 + jnp.tanh(c * (g + 0.044715 * (g * g * g))))


def _causal_conv(u_ref, r, cw, cb):
    base = SUBLANES + r
    u0 = u_ref[base:base + FFN_ROWS, :]
    u1 = u_ref[base - 1:base - 1 + FFN_ROWS, :]
    u2 = u_ref[base - 2:base - 2 + FFN_ROWS, :]
    return cb + cw[0:1] * u2 + cw[1:2] * u1 + cw[2:3] * u0


def _conv_ffn_kernel(x_ref, pre_ref, wg_ref, wv_ref, cwg_ref, cwv_ref, cbg_ref, cbv_ref, wo_ref,
                     post_ref, out_ref, h_ref, acc_ref, tail_ref, u_ref):
    t = pl.program_id(1)
    c = pl.program_id(2)
    nc = pl.num_programs(2)
    tm = x_ref.shape[1]
    fc = wg_ref.shape[1]

    @pl.when(c == 0)
    def _():
        h_ref[...] = _rms(x_ref[0], pre_ref[...]).astype(BF16)
        acc_ref[...] = jnp.zeros(acc_ref.shape, F32)

    @pl.when(t == 0)
    def _():
        tail_ref[c] = jnp.zeros(tail_ref.shape[1:], F32)

    subs = list(range(0, fc, FFN_SUB))
    for si, lo in enumerate(subs):
        u_ref[si, 0, 0:SUBLANES, :] = tail_ref[c, 0, :, lo:lo + FFN_SUB]
        u_ref[si, 1, 0:SUBLANES, :] = tail_ref[c, 1, :, lo:lo + FFN_SUB]

    groups = [(si, r) for si in range(len(subs)) for r in range(0, tm, FFN_ROWS)]

    def up(si, r):
        lo = subs[si]
        hr = h_ref[r:r + FFN_ROWS, :]
        rows = slice(SUBLANES + r, SUBLANES + r + FFN_ROWS)
        u_ref[si, 0, rows, :] = _dot(hr, wg_ref[:, lo:lo + FFN_SUB])
        u_ref[si, 1, rows, :] = _dot(hr, wv_ref[:, lo:lo + FFN_SUB])

    up(*groups[0])
    for i, (si, r) in enumerate(groups):
        lo, hi = subs[si], subs[si] + FFN_SUB
        if i + 1 < len(groups):
            up(*groups[i + 1])
        cg = _causal_conv(u_ref.at[si, 0], r, cwg_ref[:, lo:hi], cbg_ref[:, lo:hi])
        cv = _causal_conv(u_ref.at[si, 1], r, cwv_ref[:, lo:hi], cbv_ref[:, lo:hi])
        y = (_gelu_tanh(cg) * cv).astype(BF16)
        acc_ref[r:r + FFN_ROWS, :] += _dot(y, wo_ref[lo:hi, :])

    for si, lo in enumerate(subs):
        tail_ref[c, 0, :, lo:lo + FFN_SUB] = u_ref[si, 0, tm:tm + SUBLANES, :]
        tail_ref[c, 1, :, lo:lo + FFN_SUB] = u_ref[si, 1, tm:tm + SUBLANES, :]

    @pl.when(c == nc - 1)
    def _():
        for r in range(0, tm, NORM_ROWS):
            rows = slice(r, r + NORM_ROWS)
            out_ref[0, rows, :] = x_ref[0, rows, :] + _rms(acc_ref[rows, :], post_ref[...])


def _conv_ffn(x, pre, w_in, conv_w, conv_b, w_out, post, tm, fc):
    b, s, d = x.shape
    nc = D_FF // fc
    return pl.pallas_call(
        _conv_ffn_kernel,
        grid=(b, s // tm, nc),
        in_specs=[
            pl.BlockSpec((1, tm, d), lambda i, t, c: (i, t, 0)),
            _const_spec(pre.shape),
            pl.BlockSpec((d, fc), lambda i, t, c: (0, c)),
            pl.BlockSpec((d, fc), lambda i, t, c: (0, c + nc)),
            pl.BlockSpec((CONV_WIDTH, fc), lambda i, t, c: (0, c)),
            pl.BlockSpec((CONV_WIDTH, fc), lambda i, t, c: (0, c + nc)),
            pl.BlockSpec((1, fc), lambda i, t, c: (0, c)),
            pl.BlockSpec((1, fc), lambda i, t, c: (0, c + nc)),
            pl.BlockSpec((fc, d), lambda i, t, c: (c, 0)),
            _const_spec(post.shape),
        ],
        out_specs=pl.BlockSpec((1, tm, d), lambda i, t, c: (i, t, 0)),
        out_shape=jax.ShapeDtypeStruct((b, s, d), F32),
        scratch_shapes=[
            pltpu.VMEM((tm, d), BF16),
            pltpu.VMEM((tm, d), F32),
            pltpu.VMEM((nc, 2, SUBLANES, fc), F32),
            pltpu.VMEM((fc // FFN_SUB, 2, SUBLANES + tm, FFN_SUB), F32),
        ],
        compiler_params=_params(3),
        name="conv_ffn",
    )(x, pre, w_in, w_in, conv_w, conv_w, conv_b, conv_b, w_out, post)


def _b_proj_kernel(x_ref, pos_ref, gs_ref, gb_ref, w_k_ref, w_vT_ref, w_qT_ref, w_qm_ref,
                   mkT_ref, mv_ref, q1T_ref, q2T_ref, k1_ref, k2_ref, vT_ref, om_ref):
    xf = x_ref[0]
    n = xf * lax.rsqrt(jnp.mean(xf * xf, axis=-1, keepdims=True) + NORM_EPS)
    hs = (n * gs_ref[...]).astype(BF16)
    hb = (n * gb_ref[...]).astype(BF16)

    half = DIFF_DIM // 2
    cos_r, sin_r = _rope_tables_rows(pos_ref[0], half, DIFF_DIM)

    tab_c = jnp.concatenate([cos_r, cos_r], axis=0).T
    tab_s = jnp.concatenate([-sin_r, sin_r], axis=0).T
    kk = _dot(hs, w_k_ref[...])
    vT = _dot_nt(w_vT_ref[...], hs)
    nk = DIFF_HEADS * DIFF_DIM
    for hd in range(DIFF_HEADS):
        for which, ref in ((0, k1_ref), (1, k2_ref)):
            base = which * nk + hd * DIFF_DIM
            xr = kk[:, base:base + DIFF_DIM]
            ref[0, hd] = (xr * tab_c + pltpu.roll(xr, half, 1) * tab_s).astype(BF16)
        vT_ref[0, hd, 0, 0:DIFF_V, :] = vT[hd * DIFF_V:(hd + 1) * DIFF_V].astype(BF16)
        vT_ref[0, hd, 0, DIFF_V:DIFF_V + SUM_ROWS, :] = jnp.ones((SUM_ROWS, x_ref.shape[1]), BF16)

    scale = DIFF_DIM ** -0.5 * LOG2_E
    cos_q, sin_q = cos_r * scale, sin_r * scale
    qT = _dot_nt(w_qT_ref[...], hb)
    for hd in range(DIFF_HEADS):
        for which, ref in ((0, q1T_ref), (1, q2T_ref)):
            base = hd * 2 * DIFF_DIM + which * DIFF_DIM
            r1, r2 = _rope_rows(qT[base:base + half], qT[base + half:base + DIFF_DIM], cos_q, sin_q)
            ref[0, hd, 0] = jnp.concatenate([r1, r2], axis=0).astype(BF16)

    _mem_attention(_dot(hb, w_qm_ref[...]), mkT_ref, mv_ref, om_ref)


def _b_proj(x, pos_row, gs, gb, w_k, w_vT, w_qT, w_qm, mkT, mv, tm, tq):
    b, s, d = x.shape
    nt = s // tm
    r = tq // tm
    m = mkT.shape[-1]
    qT_shape = jax.ShapeDtypeStruct((b, DIFF_HEADS, s // tq, DIFF_DIM, tq), BF16)
    k_shape = jax.ShapeDtypeStruct((b, DIFF_HEADS, s, DIFF_DIM), BF16)
    qT_spec = pl.BlockSpec((1, DIFF_HEADS, 1, DIFF_DIM, tm), lambda i, t: (i, 0, t // r, 0, t % r))
    k_spec = pl.BlockSpec((1, DIFF_HEADS, tm, DIFF_DIM), lambda i, t: (i, 0, t, 0))
    return pl.pallas_call(
        _b_proj_kernel,
        grid=(b, nt),
        in_specs=[
            pl.BlockSpec((1, tm, d), lambda i, t: (i, t, 0)),
            pl.BlockSpec((1, 1, tm), lambda i, t: (i, 0, t)),
            _const_spec(gs.shape),
            _const_spec(gb.shape),
            _const_spec(w_k.shape),
            _const_spec(w_vT.shape),
            _const_spec(w_qT.shape),
            _const_spec(w_qm.shape),
            pl.BlockSpec((1, 1, MEM_HEADS, MEM_DIM, m), lambda i, t: (1, i, 0, 0, 0)),
            pl.BlockSpec((1, 1, MEM_HEADS, m, MEM_DIM), lambda i, t: (1, i, 0, 0, 0)),
        ],
        out_specs=[
            qT_spec, qT_spec, k_spec, k_spec,
            pl.BlockSpec((1, DIFF_HEADS, 1, DIFF_V + SUM_ROWS, tm), lambda i, t: (i, 0, t, 0, 0)),
            pl.BlockSpec((1, tm, MEM_HEADS * MEM_DIM), lambda i, t: (i, t, 0)),
        ],
        out_shape=[
            qT_shape, qT_shape, k_shape, k_shape,
            jax.ShapeDtypeStruct((b, DIFF_HEADS, nt, DIFF_V + SUM_ROWS, tm), BF16),
            jax.ShapeDtypeStruct((b, s, MEM_HEADS * MEM_DIM), BF16),
        ],
        compiler_params=_params(2),
        name="b_proj",
    )(x, pos_row, gs, gb, w_k, w_vT, w_qT, w_qm, mkT, mv)


def _row(v):
    return v.reshape(1, -1).astype(F32)


def kernel(x, mem, positions, a_attn_norm_pre, a_w_in, a_q_norm, a_w_uq, a_kv_norm, a_w_ukv, a_mem_norm, a_w_mem_kv, a_w_o, a_attn_norm_post, shared_kv_norm, shared_w_kv, b_attn_norm_pre, b_w_in, b_lambda_q1, b_lambda_k1, b_lambda_q2, b_lambda_k2, b_subln, b_mem_norm, b_w_mem_kv, b_w_o, b_attn_norm_post, ffn_norm_pre, ffn_w_in, ffn_conv_w, ffn_conv_b, ffn_w_out, ffn_norm_post):
    b, s, d = x.shape
    assert d == D_MODEL and a_attn_norm_pre.shape[0] == 1 and b_attn_norm_pre.shape[0] == 1
    tm, tq, fc = _tiles(s)
    assert s % tq == 0 and tq % tm == 0 and D_FF % fc == 0
    pos_row = positions.reshape(b, 1, s)

    mem_gains = jnp.stack([a_mem_norm[0], b_mem_norm[0]]).reshape(2, 1, d)
    mem_w = jnp.stack([a_w_mem_kv[0], b_w_mem_kv[0]]).astype(BF16)
    mkT, mv = _mem_kv(mem, mem_gains, mem_w)

    w_in = a_w_in[0]
    i2 = Q_LORA + KV_LORA + MLA_ROPE
    w_in_p = jnp.concatenate(
        [w_in[:, :i2], jnp.zeros((d, LANES - MLA_ROPE), F32), w_in[:, i2:]], axis=1).astype(BF16)
    w_uqT = a_w_uq[0].T.astype(BF16)
    w_ukv = a_w_ukv[0].reshape(KV_LORA, MLA_HEADS, MLA_NOPE + MLA_V)
    w_k = w_ukv[:, :, :MLA_NOPE].reshape(KV_LORA, MLA_HEADS * MLA_NOPE).astype(BF16)
    w_vT = w_ukv[:, :, MLA_NOPE:].reshape(KV_LORA, MLA_HEADS * MLA_V).T.astype(BF16)
    qT, k, vT, om = _a_proj(x, pos_row, _row(a_attn_norm_pre[0]), w_in_p, _row(a_q_norm[0]), w_uqT,
                            _row(a_kv_norm[0]), w_k, w_vT, mkT, mv, tm, tq)
    o = _mla_attn(qT, k, vT)
    n_main = MLA_HEADS * MLA_V
    w_o = a_w_o[0].astype(BF16)
    x = _out_proj(o, om, x, w_o[:n_main], w_o[n_main:], _row(a_attn_norm_post[0]), tm)
    x = _conv_ffn(x, _row(ffn_norm_pre[0]), ffn_w_in[0].astype(BF16), ffn_conv_w[0],
                  _row(ffn_conv_b[0]), ffn_w_out[0].astype(BF16), _row(ffn_norm_post[0]), tm, fc)

    layer = N_A_LAYERS
    lam_init = 0.8 - 0.6 * math.exp(-0.3 * layer)
    w_kv = shared_w_kv.reshape(d, DIFF_HEADS, 2 * DIFF_DIM + DIFF_V)
    nk = DIFF_HEADS * DIFF_DIM
    w_k12 = jnp.concatenate(
        [w_kv[:, :, :DIFF_DIM].reshape(d, nk), w_kv[:, :, DIFF_DIM:2 * DIFF_DIM].reshape(d, nk)],
        axis=1).astype(BF16)
    w_svT = w_kv[:, :, 2 * DIFF_DIM:].reshape(d, DIFF_HEADS * DIFF_V).T.astype(BF16)
    nq_cols = DIFF_HEADS * 2 * DIFF_DIM
    w_bq_T = b_w_in[0][:, :nq_cols].T.astype(BF16)
    w_bqm = b_w_in[0][:, nq_cols:].astype(BF16)
    q1T, q2T, k1, k2, vbT, omb = _b_proj(x, pos_row, _row(shared_kv_norm), _row(b_attn_norm_pre[0]),
                                         w_k12, w_svT, w_bq_T, w_bqm, mkT, mv, tm, tq)
    lam_params = jnp.stack([b_lambda_q1[0], b_lambda_k1[0], b_lambda_q2[0], b_lambda_k2[0]]).astype(F32)
    ob = _diff_attn(q1T, q2T, k1, k2, vbT, lam_params, _row(b_subln[0]), lam_init)
    n_main_b = DIFF_HEADS * DIFF_V
    w_ob = b_w_o[0].astype(BF16)
    x = _out_proj(ob, omb, x, w_ob[:n_main_b], w_ob[n_main_b:], _row(b_attn_norm_post[0]), tm)
    x = _conv_ffn(x, _row(ffn_norm_pre[1]), ffn_w_in[1].astype(BF16), ffn_conv_w[1],
                  _row(ffn_conv_b[1]), ffn_w_out[1].astype(BF16), _row(ffn_norm_post[1]), tm, fc)
    return x
```

```python
import functools
import math

import jax
import jax.numpy as jnp
from jax import lax
from jax.experimental import pallas as pl
from jax.experimental.pallas import tpu as pltpu

D_MODEL = 2048
ROPE_THETA = 10000.0
NORM_EPS = 1e-6
MLA_HEADS = 12
MLA_NOPE = 128
MLA_ROPE = 64
MLA_V = 128
MLA_DK = MLA_NOPE + MLA_ROPE
Q_LORA = 512
KV_LORA = 512
DIFF_HEADS = 6
DIFF_DIM = 128
DIFF_V = 2 * DIFF_DIM
MEM_HEADS = 4
MEM_DIM = 128
D_FF = 5632
CONV_WIDTH = 3
N_A_LAYERS = 1

LANES = 128
SUBLANES = 8
SUM_ROWS = 16
A_IN_PAD = Q_LORA + KV_LORA + LANES + MEM_HEADS * MEM_DIM

V7X_VMEM_BYTES = 64 * 1024 * 1024
VMEM_LIMIT = V7X_VMEM_BYTES - 8 * 1024 * 1024

NEG_BIG = -1e30
LOG2_E = math.log2(math.e)
BF16 = jnp.bfloat16
F32 = jnp.float32


def _tiles(seq):
    tm = min(512, seq)
    tq = min(1024, seq)
    fc = 512
    return tm, tq, fc


def _params(n_axes):
    return pltpu.CompilerParams(dimension_semantics=("arbitrary",) * n_axes,
                                vmem_limit_bytes=VMEM_LIMIT)


def _const_spec(shape):
    nd = len(shape)
    return pl.BlockSpec(shape, lambda *_: (0,) * nd, pipeline_mode=pl.Buffered(1))


def _rms(xf, gain):
    ms = jnp.mean(xf * xf, axis=-1, keepdims=True)
    return xf * lax.rsqrt(ms + NORM_EPS) * gain


def _dot(a, b):
    return jnp.dot(a, b, preferred_element_type=F32)


def _dot_nt(a, b):
    return lax.dot_general(a, b, (((1,), (1,)), ((), ())), preferred_element_type=F32)


def _rope_tables_rows(pos_row, half, dh):
    tokens = pos_row.shape[-1]
    j = lax.broadcasted_iota(jnp.int32, (half, tokens), 0).astype(F32)
    inv_freq = jnp.exp(j * (-math.log(ROPE_THETA) * 2.0 / dh))
    ang = pos_row.astype(F32) * inv_freq
    return jnp.cos(ang), jnp.sin(ang)


def _rope_rows(a, b, cos_r, sin_r):
    return a * cos_r - b * sin_r, b * cos_r + a * sin_r


def _mem_attention(qm, mkT_ref, mv_ref, out_ref):
    scale = MEM_DIM ** -0.5
    for h in range(MEM_HEADS):
        qh = (qm[:, h * MEM_DIM:(h + 1) * MEM_DIM] * scale).astype(BF16)
        s = _dot(qh, mkT_ref[0, 0, h])
        m = jnp.max(s, axis=-1, keepdims=True)
        p = jnp.exp(s - m)
        l = jnp.sum(p, axis=-1, keepdims=True)
        o = _dot(p.astype(BF16), mv_ref[0, 0, h])
        out_ref[0, :, h * MEM_DIM:(h + 1) * MEM_DIM] = (o / l).astype(out_ref.dtype)


def _mem_kv_kernel(mem_ref, g_ref, w_ref, kT_ref, v_ref):
    n = _rms(mem_ref[0], g_ref[0]).astype(BF16)
    kv = _dot(n, w_ref[0])
    for h in range(MEM_HEADS):
        base = h * 2 * MEM_DIM
        kT_ref[0, 0, h] = kv[:, base:base + MEM_DIM].T.astype(BF16)
        v_ref[0, 0, h] = kv[:, base + MEM_DIM:base + 2 * MEM_DIM].astype(BF16)


def _mem_kv(mem, gains, weights):
    b, m, d = mem.shape
    nl = gains.shape[0]
    return pl.pallas_call(
        _mem_kv_kernel,
        grid=(nl, b),
        in_specs=[
            pl.BlockSpec((1, m, d), lambda l, i: (i, 0, 0)),
            pl.BlockSpec((1, 1, d), lambda l, i: (l, 0, 0)),
            pl.BlockSpec((1, d, MEM_HEADS * 2 * MEM_DIM), lambda l, i: (l, 0, 0)),
        ],
        out_specs=[
            pl.BlockSpec((1, 1, MEM_HEADS, MEM_DIM, m), lambda l, i: (l, i, 0, 0, 0)),
            pl.BlockSpec((1, 1, MEM_HEADS, m, MEM_DIM), lambda l, i: (l, i, 0, 0, 0)),
        ],
        out_shape=[
            jax.ShapeDtypeStruct((nl, b, MEM_HEADS, MEM_DIM, m), BF16),
            jax.ShapeDtypeStruct((nl, b, MEM_HEADS, m, MEM_DIM), BF16),
        ],
        compiler_params=_params(2),
        name="mem_kv",
    )(mem, gains, weights)


def _a_proj_kernel(x_ref, pos_ref, g_ref, w_in_ref, qg_ref, w_uqT_ref, kvg_ref, w_k_ref, w_vT_ref,
                   mkT_ref, mv_ref, qT_ref, k_ref, vT_ref, om_ref):
    tm = x_ref.shape[1]
    h = _rms(x_ref[0], g_ref[...]).astype(BF16)
    proj = _dot(h, w_in_ref[...])
    ql = _rms(proj[:, :Q_LORA], qg_ref[...]).astype(BF16)
    kvl = _rms(proj[:, Q_LORA:Q_LORA + KV_LORA], kvg_ref[...]).astype(BF16)
    pe = proj[:, Q_LORA + KV_LORA:Q_LORA + KV_LORA + LANES]
    qm = proj[:, Q_LORA + KV_LORA + LANES:]

    half = MLA_ROPE // 2
    cos_r, sin_r = _rope_tables_rows(pos_ref[0], half, MLA_ROPE)

    peT = pe.T
    r1, r2 = _rope_rows(peT[0:half], peT[half:2 * half], cos_r, sin_r)
    kpe = jnp.concatenate([r1, r2, jnp.zeros((LANES - MLA_ROPE, tm), F32)], axis=0).T
    kpe = kpe[:, :MLA_ROPE].astype(BF16)

    scale = MLA_DK ** -0.5 * LOG2_E
    cos_q, sin_q = cos_r * scale, sin_r * scale
    qT = _dot_nt(w_uqT_ref[...], ql)
    kn = _dot(kvl, w_k_ref[...])
    vT = _dot_nt(w_vT_ref[...], kvl)
    for hd in range(MLA_HEADS):
        base = hd * MLA_DK
        qT_ref[0, hd, 0, 0:MLA_NOPE, :] = (qT[base:base + MLA_NOPE] * scale).astype(BF16)
        r1, r2 = _rope_rows(qT[base + MLA_NOPE:base + MLA_NOPE + half],
                            qT[base + MLA_NOPE + half:base + MLA_DK], cos_q, sin_q)
        qT_ref[0, hd, 0, MLA_NOPE:MLA_NOPE + half, :] = r1.astype(BF16)
        qT_ref[0, hd, 0, MLA_NOPE + half:MLA_DK, :] = r2.astype(BF16)
        k_ref[0, hd, :, 0:MLA_NOPE] = kn[:, hd * MLA_NOPE:(hd + 1) * MLA_NOPE].astype(BF16)
        k_ref[0, hd, :, MLA_NOPE:MLA_DK] = kpe
        vT_ref[0, hd, 0, 0:MLA_V, :] = vT[hd * MLA_V:(hd + 1) * MLA_V].astype(BF16)
        vT_ref[0, hd, 0, MLA_V:MLA_V + SUM_ROWS, :] = jnp.ones((SUM_ROWS, tm), BF16)

    _mem_attention(qm, mkT_ref, mv_ref, om_ref)


def _a_proj(x, pos_row, gain, w_in, qg, w_uqT, kvg, w_k, w_vT, mkT, mv, tm, tq):
    b, s, d = x.shape
    nt = s // tm
    r = tq // tm
    m = mkT.shape[-1]
    return pl.pallas_call(
        _a_proj_kernel,
        grid=(b, nt),
        in_specs=[
            pl.BlockSpec((1, tm, d), lambda i, t: (i, t, 0)),
            pl.BlockSpec((1, 1, tm), lambda i, t: (i, 0, t)),
            _const_spec(gain.shape),
            _const_spec(w_in.shape),
            _const_spec(qg.shape),
            _const_spec(w_uqT.shape),
            _const_spec(kvg.shape),
            _const_spec(w_k.shape),
            _const_spec(w_vT.shape),
            pl.BlockSpec((1, 1, MEM_HEADS, MEM_DIM, m), lambda i, t: (0, i, 0, 0, 0)),
            pl.BlockSpec((1, 1, MEM_HEADS, m, MEM_DIM), lambda i, t: (0, i, 0, 0, 0)),
        ],
        out_specs=[
            pl.BlockSpec((1, MLA_HEADS, 1, MLA_DK, tm), lambda i, t: (i, 0, t // r, 0, t % r)),
            pl.BlockSpec((1, MLA_HEADS, tm, MLA_DK), lambda i, t: (i, 0, t, 0)),
            pl.BlockSpec((1, MLA_HEADS, 1, MLA_V + SUM_ROWS, tm), lambda i, t: (i, 0, t, 0, 0)),
            pl.BlockSpec((1, tm, MEM_HEADS * MEM_DIM), lambda i, t: (i, t, 0)),
        ],
        out_shape=[
            jax.ShapeDtypeStruct((b, MLA_HEADS, s // tq, MLA_DK, tq), BF16),
            jax.ShapeDtypeStruct((b, MLA_HEADS, s, MLA_DK), BF16),
            jax.ShapeDtypeStruct((b, MLA_HEADS, nt, MLA_V + SUM_ROWS, tm), BF16),
            jax.ShapeDtypeStruct((b, s, MEM_HEADS * MEM_DIM), BF16),
        ],
        compiler_params=_params(2),
        name="a_proj",
    )(x, pos_row, gain, w_in, qg, w_uqT, kvg, w_k, w_vT, mkT, mv)


def _online_softmax_t(s, m_prev, masked):
    if masked:
        key = lax.broadcasted_iota(jnp.int32, s.shape, 0)
        qry = lax.broadcasted_iota(jnp.int32, s.shape, 1)
        s = jnp.where(key <= qry, s, NEG_BIG)
    m_new = jnp.maximum(m_prev, jnp.max(s, axis=0, keepdims=True))
    alpha = jnp.exp2(m_prev - m_new)
    p = jnp.exp2(s - m_new)
    return p.astype(BF16), m_new, alpha


def _fold_key_block(s_ref, acc_ref, vT, m_prev, q_lo):
    if q_lo is None:
        p, m_new, alpha = _online_softmax_t(s_ref[...], m_prev, False)
        acc_ref[...] = alpha * acc_ref[...] + _dot(vT, p)
        return m_new
    p, m_part, alpha = _online_softmax_t(s_ref[:, q_lo:], m_prev[:, q_lo:], True)
    acc_ref[:, q_lo:] = alpha * acc_ref[:, q_lo:] + _dot(vT, p)
    return m_part if q_lo == 0 else jnp.concatenate([m_prev[:, :q_lo], m_part], axis=1)


KEY_BLOCK_UNROLL = 4


def _pipelined_key_blocks(qi, nd, tk, scores, consume, finish, init):
    n_full = qi * nd
    scores(0, 0, None)

    def group(g, carry):
        j = KEY_BLOCK_UNROLL * g
        for u in range(KEY_BLOCK_UNROLL):
            scores(j + u + 1, (u + 1) % 2, None)
            carry = consume(u % 2, j + u, carry, None)
        return carry

    carry = lax.fori_loop(0, n_full // KEY_BLOCK_UNROLL, group, init)
    base = (n_full // KEY_BLOCK_UNROLL) * KEY_BLOCK_UNROLL

    for rem in range(0, KEY_BLOCK_UNROLL, math.gcd(nd, KEY_BLOCK_UNROLL)):
        @pl.when(n_full % KEY_BLOCK_UNROLL == rem)
        def _(rem=rem):
            blocks = [(base + u, None) for u in range(rem)] + [(n_full + d, d * tk) for d in range(nd)]
            c = carry
            for idx, (j, q_lo) in enumerate(blocks):
                if idx + 1 < len(blocks):
                    scores(blocks[idx + 1][0], (idx + 1) % 2, blocks[idx + 1][1])
                c = consume(idx % 2, j, c, q_lo)
            finish(c)


def _mla_attn_kernel(qT_ref, k_ref, vT_ref, o_ref, sa_ref, sb_ref, acc_ref):
    tq = qT_ref.shape[4]
    tk = vT_ref.shape[4]
    qi = pl.program_id(2)
    qT = qT_ref[0, 0, 0]
    s_refs = (sa_ref, sb_ref)

    def scores(j, slot, q_lo):
        off = pl.multiple_of(j * tk, tk)
        lo = q_lo or 0
        s_refs[slot][:, lo:] = _dot(k_ref[0, 0, pl.ds(off, tk), :], qT[:, lo:])

    def consume(slot, j, carry, q_lo):
        return _fold_key_block(s_refs[slot], acc_ref, vT_ref[0, 0, j], carry, q_lo)

    def finish(carry):
        l = acc_ref[MLA_V:MLA_V + 1, :]
        o_ref[0] = (acc_ref[0:MLA_V, :] / l).T.astype(o_ref.dtype)

    acc_ref[...] = jnp.zeros(acc_ref.shape, F32)
    init = jnp.full((1, tq), NEG_BIG, F32)
    _pipelined_key_blocks(qi, tq // tk, tk, scores, consume, finish, init)


def _mla_attn(qT, k, vT):
    b, nh, nq, _, tq = qT.shape
    nkb, tk = vT.shape[2], vT.shape[4]
    s = nq * tq
    return pl.pallas_call(
        _mla_attn_kernel,
        grid=(b, nh, nq),
        in_specs=[
            pl.BlockSpec((1, 1, 1, MLA_DK, tq), lambda i, h, t: (i, h, t, 0, 0)),
            pl.BlockSpec((1, 1, s, MLA_DK), lambda i, h, t: (i, h, 0, 0)),
            pl.BlockSpec((1, 1, nkb, MLA_V + SUM_ROWS, tk), lambda i, h, t: (i, h, 0, 0, 0)),
        ],
        out_specs=pl.BlockSpec((1, tq, MLA_V), lambda i, h, t: (i, t, h)),
        out_shape=jax.ShapeDtypeStruct((b, s, nh * MLA_V), BF16),
        scratch_shapes=[pltpu.VMEM((tk, tq), F32), pltpu.VMEM((tk, tq), F32),
                        pltpu.VMEM((MLA_V + SUM_ROWS, tq), F32)],
        compiler_params=_params(3),
        name="mla_attn",
    )(qT, k, vT)


def _diff_attn_kernel(lam_init, q1T_ref, q2T_ref, k1_ref, k2_ref, vT_ref, lam_ref, sub_ref, o_ref,
                      s1a_ref, s1b_ref, s2a_ref, s2b_ref, acc1_ref, acc2_ref):
    tq = q1T_ref.shape[4]
    tk = vT_ref.shape[4]
    qi = pl.program_id(2)
    q1T = q1T_ref[0, 0, 0]
    q2T = q2T_ref[0, 0, 0]
    s1_refs = (s1a_ref, s1b_ref)
    s2_refs = (s2a_ref, s2b_ref)

    def scores(j, slot, q_lo):
        off = pl.multiple_of(j * tk, tk)
        lo = q_lo or 0
        s1_refs[slot][:, lo:] = _dot(k1_ref[0, 0, pl.ds(off, tk), :], q1T[:, lo:])
        s2_refs[slot][:, lo:] = _dot(k2_ref[0, 0, pl.ds(off, tk), :], q2T[:, lo:])

    def consume(slot, j, carry, q_lo):
        vT = vT_ref[0, 0, j]
        m1 = _fold_key_block(s1_refs[slot], acc1_ref, vT, carry[0], q_lo)
        m2 = _fold_key_block(s2_refs[slot], acc2_ref, vT, carry[1], q_lo)
        return m1, m2

    def finish(carry):
        l1 = acc1_ref[DIFF_V:DIFF_V + 1, :]
        l2 = acc2_ref[DIFF_V:DIFF_V + 1, :]
        lp = lam_ref[...]
        lam = (jnp.exp(jnp.sum(lp[0:1] * lp[1:2], axis=-1, keepdims=True))
               - jnp.exp(jnp.sum(lp[2:3] * lp[3:4], axis=-1, keepdims=True)) + lam_init)
        oT = acc1_ref[0:DIFF_V, :] / l1 - lam * (acc2_ref[0:DIFF_V, :] / l2)
        o_ref[0] = (_rms(oT.T, sub_ref[...]) * (1.0 - lam_init)).astype(o_ref.dtype)

    acc1_ref[...] = jnp.zeros(acc1_ref.shape, F32)
    acc2_ref[...] = jnp.zeros(acc2_ref.shape, F32)
    m_init = jnp.full((1, tq), NEG_BIG, F32)
    _pipelined_key_blocks(qi, tq // tk, tk, scores, consume, finish, (m_init, m_init))


def _diff_attn(q1T, q2T, k1, k2, vT, lam_params, subln, lam_init):
    b, nh, nq, _, tq = q1T.shape
    nkb, tk = vT.shape[2], vT.shape[4]
    s = nq * tq
    qT_spec = pl.BlockSpec((1, 1, 1, DIFF_DIM, tq), lambda i, h, t: (i, h, t, 0, 0))
    k_spec = pl.BlockSpec((1, 1, s, DIFF_DIM), lambda i, h, t: (i, h, 0, 0))
    return pl.pallas_call(
        functools.partial(_diff_attn_kernel, lam_init),
        grid=(b, nh, nq),
        in_specs=[
            qT_spec, qT_spec, k_spec, k_spec,
            pl.BlockSpec((1, 1, nkb, DIFF_V + SUM_ROWS, tk), lambda i, h, t: (i, h, 0, 0, 0)),
            _const_spec(lam_params.shape),
            _const_spec(subln.shape),
        ],
        out_specs=pl.BlockSpec((1, tq, DIFF_V), lambda i, h, t: (i, t, h)),
        out_shape=jax.ShapeDtypeStruct((b, s, nh * DIFF_V), BF16),
        scratch_shapes=[pltpu.VMEM((tk, tq), F32)] * 4 + [pltpu.VMEM((DIFF_V + SUM_ROWS, tq), F32)] * 2,
        compiler_params=_params(3),
        name="diff_attn",
    )(q1T, q2T, k1, k2, vT, lam_params, subln)


def _out_proj_kernel(o_ref, om_ref, x_ref, w1_ref, w2_ref, g_ref, out_ref):
    mix = _dot(o_ref[0], w1_ref[...]) + _dot(om_ref[0], w2_ref[...])
    out_ref[0] = x_ref[0] + _rms(mix, g_ref[...])


def _out_proj(o, om, x, w1, w2, gain, tm):
    b, s, d = x.shape
    return pl.pallas_call(
        _out_proj_kernel,
        grid=(b, s // tm),
        in_specs=[
            pl.BlockSpec((1, tm, o.shape[-1]), lambda i, t: (i, t, 0)),
            pl.BlockSpec((1, tm, om.shape[-1]), lambda i, t: (i, t, 0)),
            pl.BlockSpec((1, tm, d), lambda i, t: (i, t, 0)),
            _const_spec(w1.shape),
            _const_spec(w2.shape),
            _const_spec(gain.shape),
        ],
        out_specs=pl.BlockSpec((1, tm, d), lambda i, t: (i, t, 0)),
        out_shape=jax.ShapeDtypeStruct((b, s, d), F32),
        compiler_params=_params(2),
        name="out_proj",
    )(o, om, x, w1, w2, gain)


FFN_SUB = 256
FFN_ROWS = 256
NORM_ROWS = 32


def _gelu_tanh(g):
    c = math.sqrt(2.0 / math.pi)
    return 0.5 * g * (1.0 + jnp.tanh(c * (g + 0.044715 * (g * g * g))))


def _causal_conv(u_ref, r, cw, cb):
    base = SUBLANES + r
    u0 = u_ref[base:base + FFN_ROWS, :]
    u1 = u_ref[base - 1:base - 1 + FFN_ROWS, :]
    u2 = u_ref[base - 2:base - 2 + FFN_ROWS, :]
    return cb + cw[0:1] * u2 + cw[1:2] * u1 + cw[2:3] * u0


def _conv_ffn_kernel(x_ref, pre_ref, wg_ref, wv_ref, cwg_ref, cwv_ref, cbg_ref, cbv_ref, wo_ref,
                     post_ref, out_ref, h_ref, acc_ref, tail_ref, u_ref):
    t = pl.program_id(1)
    c = pl.program_id(2)
    nc = pl.num_programs(2)
    tm = x_ref.shape[1]
    fc = wg_ref.shape[1]

    @pl.when(c == 0)
    def _():
        h_ref[...] = _rms(x_ref[0], pre_ref[...]).astype(BF16)
        acc_ref[...] = jnp.zeros(acc_ref.shape, F32)

    @pl.when(t == 0)
    def _():
        tail_ref[c] = jnp.zeros(tail_ref.shape[1:], F32)

    subs = list(range(0, fc, FFN_SUB))
    for si, lo in enumerate(subs):
        u_ref[si, 0, 0:SUBLANES, :] = tail_ref[c, 0, :, lo:lo + FFN_SUB]
        u_ref[si, 1, 0:SUBLANES, :] = tail_ref[c, 1, :, lo:lo + FFN_SUB]

    groups = [(si, r) for si in range(len(subs)) for r in range(0, tm, FFN_ROWS)]

    def up(si, r):
        lo = subs[si]
        hr = h_ref[r:r + FFN_ROWS, :]
        rows = slice(SUBLANES + r, SUBLANES + r + FFN_ROWS)
        u_ref[si, 0, rows, :] = _dot(hr, wg_ref[:, lo:lo + FFN_SUB])
        u_ref[si, 1, rows, :] = _dot(hr, wv_ref[:, lo:lo + FFN_SUB])

    for g in groups:
        up(*g)
    ys = []
    for si, r in groups:
        lo, hi = subs[si], subs[si] + FFN_SUB
        cg = _causal_conv(u_ref.at[si, 0], r, cwg_ref[:, lo:hi], cbg_ref[:, lo:hi])
        cv = _causal_conv(u_ref.at[si, 1], r, cwv_ref[:, lo:hi], cbv_ref[:, lo:hi])
        ys.append((_gelu_tanh(cg) * cv).astype(BF16))
    for (si, r), y in zip(groups, ys):
        lo, hi = subs[si], subs[si] + FFN_SUB
        acc_ref[r:r + FFN_ROWS, :] += _dot(y, wo_ref[lo:hi, :])

    for si, lo in enumerate(subs):
        tail_ref[c, 0, :, lo:lo + FFN_SUB] = u_ref[si, 0, tm:tm + SUBLANES, :]
        tail_ref[c, 1, :, lo:lo + FFN_SUB] = u_ref[si, 1, tm:tm + SUBLANES, :]

    @pl.when(c == nc - 1)
    def _():
        for r in range(0, tm, NORM_ROWS):
            rows = slice(r, r + NORM_ROWS)
            out_ref[0, rows, :] = x_ref[0, rows, :] + _rms(acc_ref[rows, :], post_ref[...])


def _conv_ffn(x, pre, w_in, conv_w, conv_b, w_out, post, tm, fc):
    b, s, d = x.shape
    nc = D_FF // fc
    return pl.pallas_call(
        _conv_ffn_kernel,
        grid=(b, s // tm, nc),
        in_specs=[
            pl.BlockSpec((1, tm, d), lambda i, t, c: (i, t, 0)),
            _const_spec(pre.shape),
            pl.BlockSpec((d, fc), lambda i, t, c: (0, c)),
            pl.BlockSpec((d, fc), lambda i, t, c: (0, c + nc)),
            pl.BlockSpec((CONV_WIDTH, fc), lambda i, t, c: (0, c)),
            pl.BlockSpec((CONV_WIDTH, fc), lambda i, t, c: (0, c + nc)),
            pl.BlockSpec((1, fc), lambda i, t, c: (0, c)),
            pl.BlockSpec((1, fc), lambda i, t, c: (0, c + nc)),
            pl.BlockSpec((fc, d), lambda i, t, c: (c, 0)),
            _const_spec(post.shape),
        ],
        out_specs=pl.BlockSpec((1, tm, d), lambda i, t, c: (i, t, 0)),
        out_shape=jax.ShapeDtypeStruct((b, s, d), F32),
        scratch_shapes=[
            pltpu.VMEM((tm, d), BF16),
            pltpu.VMEM((tm, d), F32),
            pltpu.VMEM((nc, 2, SUBLANES, fc), F32),
            pltpu.VMEM((fc // FFN_SUB, 2, SUBLANES + tm, FFN_SUB), F32),
        ],
        compiler_params=_params(3),
        name="conv_ffn",
    )(x, pre, w_in, w_in, conv_w, conv_w, conv_b, conv_b, w_out, post)


def _b_proj_kernel(x_ref, pos_ref, gs_ref, gb_ref, w_k_ref, w_vT_ref, w_qT_ref, w_qm_ref,
                   mkT_ref, mv_ref, q1T_ref, q2T_ref, k1_ref, k2_ref, vT_ref, om_ref):
    xf = x_ref[0]
    n = xf * lax.rsqrt(jnp.mean(xf * xf, axis=-1, keepdims=True) + NORM_EPS)
    hs = (n * gs_ref[...]).astype(BF16)
    hb = (n * gb_ref[...]).astype(BF16)

    half = DIFF_DIM // 2
    cos_r, sin_r = _rope_tables_rows(pos_ref[0], half, DIFF_DIM)

    tab_c = jnp.concatenate([cos_r, cos_r], axis=0).T
    tab_s = jnp.concatenate([-sin_r, sin_r], axis=0).T
    kk = _dot(hs, w_k_ref[...])
    vT = _dot_nt(w_vT_ref[...], hs)
    nk = DIFF_HEADS * DIFF_DIM
    for hd in range(DIFF_HEADS):
        for which, ref in ((0, k1_ref), (1, k2_ref)):
            base = which * nk + hd * DIFF_DIM
            xr = kk[:, base:base + DIFF_DIM]
            ref[0, hd] = (xr * tab_c + pltpu.roll(xr, half, 1) * tab_s).astype(BF16)
        vT_ref[0, hd, 0, 0:DIFF_V, :] = vT[hd * DIFF_V:(hd + 1) * DIFF_V].astype(BF16)
        vT_ref[0, hd, 0, DIFF_V:DIFF_V + SUM_ROWS, :] = jnp.ones((SUM_ROWS, x_ref.shape[1]), BF16)

    scale = DIFF_DIM ** -0.5 * LOG2_E
    cos_q, sin_q = cos_r * scale, sin_r * scale
    qT = _dot_nt(w_qT_ref[...], hb)
    for hd in range(DIFF_HEADS):
        for which, ref in ((0, q1T_ref), (1, q2T_ref)):
            base = hd * 2 * DIFF_DIM + which * DIFF_DIM
            r1, r2 = _rope_rows(qT[base:base + half], qT[base + half:base + DIFF_DIM], cos_q, sin_q)
            ref[0, hd, 0] = jnp.concatenate([r1, r2], axis=0).astype(BF16)

    _mem_attention(_dot(hb, w_qm_ref[...]), mkT_ref, mv_ref, om_ref)


def _b_proj(x, pos_row, gs, gb, w_k, w_vT, w_qT, w_qm, mkT, mv, tm, tq):
    b, s, d = x.shape
    nt = s // tm
    r = tq // tm
    m = mkT.shape[-1]
    qT_shape = jax.ShapeDtypeStruct((b, DIFF_HEADS, s // tq, DIFF_DIM, tq), BF16)
    k_shape = jax.ShapeDtypeStruct((b, DIFF_HEADS, s, DIFF_DIM), BF16)
    qT_spec = pl.BlockSpec((1, DIFF_HEADS, 1, DIFF_DIM, tm), lambda i, t: (i, 0, t // r, 0, t % r))
    k_spec = pl.BlockSpec((1, DIFF_HEADS, tm, DIFF_DIM), lambda i, t: (i, 0, t, 0))
    return pl.pallas_call(
        _b_proj_kernel,
        grid=(b, nt),
        in_specs=[
            pl.BlockSpec((1, tm, d), lambda i, t: (i, t, 0)),
            pl.BlockSpec((1, 1, tm), lambda i, t: (i, 0, t)),
            _const_spec(gs.shape),
            _const_spec(gb.shape),
            _const_spec(w_k.shape),
            _const_spec(w_vT.shape),
            _const_spec(w_qT.shape),
            _const_spec(w_qm.shape),
            pl.BlockSpec((1, 1, MEM_HEADS, MEM_DIM, m), lambda i, t: (1, i, 0, 0, 0)),
            pl.BlockSpec((1, 1, MEM_HEADS, m, MEM_DIM), lambda i, t: (1, i, 0, 0, 0)),
        ],
        out_specs=[
            qT_spec, qT_spec, k_spec, k_spec,
            pl.BlockSpec((1, DIFF_HEADS, 1, DIFF_V + SUM_ROWS, tm), lambda i, t: (i, 0, t, 0, 0)),
            pl.BlockSpec((1, tm, MEM_HEADS * MEM_DIM), lambda i, t: (i, t, 0)),
        ],
        out_shape=[
            qT_shape, qT_shape, k_shape, k_shape,
            jax.ShapeDtypeStruct((b, DIFF_HEADS, nt, DIFF_V + SUM_ROWS, tm), BF16),
            jax.ShapeDtypeStruct((b, s, MEM_HEADS * MEM_DIM), BF16),
        ],
        compiler_params=_params(2),
        name="b_proj",
    )(x, pos_row, gs, gb, w_k, w_vT, w_qT, w_qm, mkT, mv)


def _row(v):
    return v.reshape(1, -1).astype(F32)


def kernel(x, mem, positions, a_attn_norm_pre, a_w_in, a_q_norm, a_w_uq, a_kv_norm, a_w_ukv, a_mem_norm, a_w_mem_kv, a_w_o, a_attn_norm_post, shared_kv_norm, shared_w_kv, b_attn_norm_pre, b_w_in, b_lambda_q1, b_lambda_k1, b_lambda_q2, b_lambda_k2, b_subln, b_mem_norm, b_w_mem_kv, b_w_o, b_attn_norm_post, ffn_norm_pre, ffn_w_in, ffn_conv_w, ffn_conv_b, ffn_w_out, ffn_norm_post):
    b, s, d = x.shape
    assert d == D_MODEL and a_attn_norm_pre.shape[0] == 1 and b_attn_norm_pre.shape[0] == 1
    tm, tq, fc = _tiles(s)
    assert s % tq == 0 and tq % tm == 0 and D_FF % fc == 0
    pos_row = positions.reshape(b, 1, s)

    mem_gains = jnp.stack([a_mem_norm[0], b_mem_norm[0]]).reshape(2, 1, d)
    mem_w = jnp.stack([a_w_mem_kv[0], b_w_mem_kv[0]]).astype(BF16)
    mkT, mv = _mem_kv(mem, mem_gains, mem_w)

    w_in = a_w_in[0]
    i2 = Q_LORA + KV_LORA + MLA_ROPE
    w_in_p = jnp.concatenate(
        [w_in[:, :i2], jnp.zeros((d, LANES - MLA_ROPE), F32), w_in[:, i2:]], axis=1).astype(BF16)
    w_uqT = a_w_uq[0].T.astype(BF16)
    w_ukv = a_w_ukv[0].reshape(KV_LORA, MLA_HEADS, MLA_NOPE + MLA_V)
    w_k = w_ukv[:, :, :MLA_NOPE].reshape(KV_LORA, MLA_HEADS * MLA_NOPE).astype(BF16)
    w_vT = w_ukv[:, :, MLA_NOPE:].reshape(KV_LORA, MLA_HEADS * MLA_V).T.astype(BF16)
    qT, k, vT, om = _a_proj(x, pos_row, _row(a_attn_norm_pre[0]), w_in_p, _row(a_q_norm[0]), w_uqT,
                            _row(a_kv_norm[0]), w_k, w_vT, mkT, mv, tm, tq)
    o = _mla_attn(qT, k, vT)
    n_main = MLA_HEADS * MLA_V
    w_o = a_w_o[0].astype(BF16)
    x = _out_proj(o, om, x, w_o[:n_main], w_o[n_main:], _row(a_attn_norm_post[0]), tm)
    x = _conv_ffn(x, _row(ffn_norm_pre[0]), ffn_w_in[0].astype(BF16), ffn_conv_w[0],
                  _row(ffn_conv_b[0]), ffn_w_out[0].astype(BF16), _row(ffn_norm_post[0]), tm, fc)

    layer = N_A_LAYERS
    lam_init = 0.8 - 0.6 * math.exp(-0.3 * layer)
    w_kv = shared_w_kv.reshape(d, DIFF_HEADS, 2 * DIFF_DIM + DIFF_V)
    nk = DIFF_HEADS * DIFF_DIM
    w_k12 = jnp.concatenate(
        [w_kv[:, :, :DIFF_DIM].reshape(d, nk), w_kv[:, :, DIFF_DIM:2 * DIFF_DIM].reshape(d, nk)],
        axis=1).astype(BF16)
    w_svT = w_kv[:, :, 2 * DIFF_DIM:].reshape(d, DIFF_HEADS * DIFF_V).T.astype(BF16)
    nq_cols = DIFF_HEADS * 2 * DIFF_DIM
    w_bq_T = b_w_in[0][:, :nq_cols].T.astype(BF16)
    w_bqm = b_w_in[0][:, nq_cols:].astype(BF16)
    q1T, q2T, k1, k2, vbT, omb = _b_proj(x, pos_row, _row(shared_kv_norm), _row(b_attn_norm_pre[0]),
                                         w_k12, w_svT, w_bq_T, w_bqm, mkT, mv, tm, tq)
    lam_params = jnp.stack([b_lambda_q1[0], b_lambda_k1[0], b_lambda_q2[0], b_lambda_k2[0]]).astype(F32)
    ob = _diff_attn(q1T, q2T, k1, k2, vbT, lam_params, _row(b_subln[0]), lam_init)
    n_main_b = DIFF_HEADS * DIFF_V
    w_ob = b_w_o[0].astype(BF16)
    x = _out_proj(ob, omb, x, w_ob[:n_main_b], w_ob[n_main_b:], _row(b_attn_norm_post[0]), tm)
    x = _conv_ffn(x, _row(ffn_norm_pre[1]), ffn_w_in[1].astype(BF16), ffn_conv_w[1],
                  _row(ffn_conv_b[1]), ffn_w_out[1].astype(BF16), _row(ffn_norm_post[1]), tm, fc)
    return x
```

```python
import functools
import math

import jax
import jax.numpy as jnp
from jax import lax
from jax.experimental import pallas as pl
from jax.experimental.pallas import tpu as pltpu

D_MODEL = 2048
ROPE_THETA = 10000.0
NORM_EPS = 1e-6
MLA_HEADS = 12
MLA_NOPE = 128
MLA_ROPE = 64
MLA_V = 128
MLA_DK = MLA_NOPE + MLA_ROPE
Q_LORA = 512
KV_LORA = 512
DIFF_HEADS = 6
DIFF_DIM = 128
DIFF_V = 2 * DIFF_DIM
MEM_HEADS = 4
MEM_DIM = 128
D_FF = 5632
CONV_WIDTH = 3
N_A_LAYERS = 1

LANES = 128
SUBLANES = 8
SUM_ROWS = 16
A_IN_PAD = Q_LORA + KV_LORA + LANES + MEM_HEADS * MEM_DIM

V7X_VMEM_BYTES = 64 * 1024 * 1024
VMEM_LIMIT = V7X_VMEM_BYTES - 8 * 1024 * 1024

NEG_BIG = -1e30
LOG2_E = math.log2(math.e)
BF16 = jnp.bfloat16
F32 = jnp.float32


def _tiles(seq):
    tm = min(512, seq)
    tq_mla = min(1024, seq)
    tq_diff = min(1024, seq)
    fc = 512
    return tm, tq_mla, tq_diff, fc


def _params(n_axes):
    return pltpu.CompilerParams(dimension_semantics=("arbitrary",) * n_axes,
                                vmem_limit_bytes=VMEM_LIMIT)


def _const_spec(shape):
    nd = len(shape)
    return pl.BlockSpec(shape, lambda *_: (0,) * nd, pipeline_mode=pl.Buffered(1))


def _rms(xf, gain):
    ms = jnp.mean(xf * xf, axis=-1, keepdims=True)
    return xf * lax.rsqrt(ms + NORM_EPS) * gain


def _dot(a, b):
    return jnp.dot(a, b, preferred_element_type=F32)


def _dot_nt(a, b):
    return lax.dot_general(a, b, (((1,), (1,)), ((), ())), preferred_element_type=F32)


def _rope_tables_rows(pos_row, half, dh):
    tokens = pos_row.shape[-1]
    j = lax.broadcasted_iota(jnp.int32, (half, tokens), 0).astype(F32)
    inv_freq = jnp.exp(j * (-math.log(ROPE_THETA) * 2.0 / dh))
    ang = pos_row.astype(F32) * inv_freq
    return jnp.cos(ang), jnp.sin(ang)


def _rope_rows(a, b, cos_r, sin_r):
    return a * cos_r - b * sin_r, b * cos_r + a * sin_r


def _mem_attention(qm, mkT_ref, mv_ref, out_ref):
    scale = MEM_DIM ** -0.5
    for h in range(MEM_HEADS):
        qh = (qm[:, h * MEM_DIM:(h + 1) * MEM_DIM] * scale).astype(BF16)
        s = _dot(qh, mkT_ref[0, 0, h])
        m = jnp.max(s, axis=-1, keepdims=True)
        p = jnp.exp(s - m)
        l = jnp.sum(p, axis=-1, keepdims=True)
        o = _dot(p.astype(BF16), mv_ref[0, 0, h])
        out_ref[0, :, h * MEM_DIM:(h + 1) * MEM_DIM] = (o / l).astype(out_ref.dtype)


def _mem_kv_kernel(mem_ref, g_ref, w_ref, kT_ref, v_ref):
    n = _rms(mem_ref[0], g_ref[0]).astype(BF16)
    kv = _dot(n, w_ref[0])
    for h in range(MEM_HEADS):
        base = h * 2 * MEM_DIM
        kT_ref[0, 0, h] = kv[:, base:base + MEM_DIM].T.astype(BF16)
        v_ref[0, 0, h] = kv[:, base + MEM_DIM:base + 2 * MEM_DIM].astype(BF16)


def _mem_kv(mem, gains, weights):
    b, m, d = mem.shape
    nl = gains.shape[0]
    return pl.pallas_call(
        _mem_kv_kernel,
        grid=(nl, b),
        in_specs=[
            pl.BlockSpec((1, m, d), lambda l, i: (i, 0, 0)),
            pl.BlockSpec((1, 1, d), lambda l, i: (l, 0, 0)),
            pl.BlockSpec((1, d, MEM_HEADS * 2 * MEM_DIM), lambda l, i: (l, 0, 0)),
        ],
        out_specs=[
            pl.BlockSpec((1, 1, MEM_HEADS, MEM_DIM, m), lambda l, i: (l, i, 0, 0, 0)),
            pl.BlockSpec((1, 1, MEM_HEADS, m, MEM_DIM), lambda l, i: (l, i, 0, 0, 0)),
        ],
        out_shape=[
            jax.ShapeDtypeStruct((nl, b, MEM_HEADS, MEM_DIM, m), BF16),
            jax.ShapeDtypeStruct((nl, b, MEM_HEADS, m, MEM_DIM), BF16),
        ],
        compiler_params=_params(2),
        name="mem_kv",
    )(mem, gains, weights)


def _a_proj_kernel(x_ref, pos_ref, g_ref, w_in_ref, qg_ref, w_uqT_ref, kvg_ref, w_k_ref, w_vT_ref,
                   mkT_ref, mv_ref, qT_ref, k_ref, vT_ref, om_ref):
    tm = x_ref.shape[1]
    h = _rms(x_ref[0], g_ref[...]).astype(BF16)
    proj = _dot(h, w_in_ref[...])
    ql = _rms(proj[:, :Q_LORA], qg_ref[...]).astype(BF16)
    kvl = _rms(proj[:, Q_LORA:Q_LORA + KV_LORA], kvg_ref[...]).astype(BF16)
    pe = proj[:, Q_LORA + KV_LORA:Q_LORA + KV_LORA + LANES]
    qm = proj[:, Q_LORA + KV_LORA + LANES:]

    half = MLA_ROPE // 2
    cos_r, sin_r = _rope_tables_rows(pos_ref[0], half, MLA_ROPE)

    peT = pe.T
    r1, r2 = _rope_rows(peT[0:half], peT[half:2 * half], cos_r, sin_r)
    kpe = jnp.concatenate([r1, r2, jnp.zeros((LANES - MLA_ROPE, tm), F32)], axis=0).T
    kpe = kpe[:, :MLA_ROPE].astype(BF16)

    scale = MLA_DK ** -0.5 * LOG2_E
    cos_q, sin_q = cos_r * scale, sin_r * scale
    qT = _dot_nt(w_uqT_ref[...], ql)
    kn = _dot(kvl, w_k_ref[...])
    vT = _dot_nt(w_vT_ref[...], kvl)
    for hd in range(MLA_HEADS):
        base = hd * MLA_DK
        qT_ref[0, hd, 0, 0:MLA_NOPE, :] = (qT[base:base + MLA_NOPE] * scale).astype(BF16)
        r1, r2 = _rope_rows(qT[base + MLA_NOPE:base + MLA_NOPE + half],
                            qT[base + MLA_NOPE + half:base + MLA_DK], cos_q, sin_q)
        qT_ref[0, hd, 0, MLA_NOPE:MLA_NOPE + half, :] = r1.astype(BF16)
        qT_ref[0, hd, 0, MLA_NOPE + half:MLA_DK, :] = r2.astype(BF16)
        k_ref[0, hd, :, 0:MLA_NOPE] = kn[:, hd * MLA_NOPE:(hd + 1) * MLA_NOPE].astype(BF16)
        k_ref[0, hd, :, MLA_NOPE:MLA_DK] = kpe
        vT_ref[0, hd, 0, 0:MLA_V, :] = vT[hd * MLA_V:(hd + 1) * MLA_V].astype(BF16)
        vT_ref[0, hd, 0, MLA_V:MLA_V + SUM_ROWS, :] = jnp.ones((SUM_ROWS, tm), BF16)

    _mem_attention(qm, mkT_ref, mv_ref, om_ref)


def _a_proj(x, pos_row, gain, w_in, qg, w_uqT, kvg, w_k, w_vT, mkT, mv, tm, tq):
    b, s, d = x.shape
    nt = s // tm
    r = tq // tm
    m = mkT.shape[-1]
    return pl.pallas_call(
        _a_proj_kernel,
        grid=(b, nt),
        in_specs=[
            pl.BlockSpec((1, tm, d), lambda i, t: (i, t, 0)),
            pl.BlockSpec((1, 1, tm), lambda i, t: (i, 0, t)),
            _const_spec(gain.shape),
            _const_spec(w_in.shape),
            _const_spec(qg.shape),
            _const_spec(w_uqT.shape),
            _const_spec(kvg.shape),
            _const_spec(w_k.shape),
            _const_spec(w_vT.shape),
            pl.BlockSpec((1, 1, MEM_HEADS, MEM_DIM, m), lambda i, t: (0, i, 0, 0, 0)),
            pl.BlockSpec((1, 1, MEM_HEADS, m, MEM_DIM), lambda i, t: (0, i, 0, 0, 0)),
        ],
        out_specs=[
            pl.BlockSpec((1, MLA_HEADS, 1, MLA_DK, tm), lambda i, t: (i, 0, t // r, 0, t % r)),
            pl.BlockSpec((1, MLA_HEADS, tm, MLA_DK), lambda i, t: (i, 0, t, 0)),
            pl.BlockSpec((1, MLA_HEADS, 1, MLA_V + SUM_ROWS, tm), lambda i, t: (i, 0, t, 0, 0)),
            pl.BlockSpec((1, tm, MEM_HEADS * MEM_DIM), lambda i, t: (i, t, 0)),
        ],
        out_shape=[
            jax.ShapeDtypeStruct((b, MLA_HEADS, s // tq, MLA_DK, tq), BF16),
            jax.ShapeDtypeStruct((b, MLA_HEADS, s, MLA_DK), BF16),
            jax.ShapeDtypeStruct((b, MLA_HEADS, nt, MLA_V + SUM_ROWS, tm), BF16),
            jax.ShapeDtypeStruct((b, s, MEM_HEADS * MEM_DIM), BF16),
        ],
        compiler_params=_params(2),
        name="a_proj",
    )(x, pos_row, gain, w_in, qg, w_uqT, kvg, w_k, w_vT, mkT, mv)


def _online_softmax_pv(s_ref, acc_ref, vT, cols, m_prev, masked):
    def read():
        s = s_ref[:, cols]
        if masked:
            key = lax.broadcasted_iota(jnp.int32, s.shape, 0)
            qry = lax.broadcasted_iota(jnp.int32, s.shape, 1)
            s = jnp.where(key <= qry, s, NEG_BIG)
        return s

    m_new = jnp.maximum(m_prev, jnp.max(read(), axis=0, keepdims=True))
    acc_ref[:, cols] = jnp.exp2(m_prev - m_new) * acc_ref[:, cols]
    p = jnp.exp2(read() - m_new).astype(BF16)
    acc_ref[:, cols] += _dot(vT, p)
    return m_new


def _fold_key_block(s_ref, acc_ref, vT, m_prev, q_lo):
    if q_lo is None:
        qc = min(QUERY_CHUNK, s_ref.shape[1])
        parts = []
        for c0 in range(0, s_ref.shape[1], qc):
            cols = slice(c0, c0 + qc)
            parts.append(_online_softmax_pv(s_ref, acc_ref, vT, cols, m_prev[:, cols], False))
        return jnp.concatenate(parts, axis=1)
    m_part = _online_softmax_pv(s_ref, acc_ref, vT, slice(q_lo, None), m_prev[:, q_lo:], True)
    return m_part if q_lo == 0 else jnp.concatenate([m_prev[:, :q_lo], m_part], axis=1)


QUERY_CHUNK = 512
KEY_BLOCK_UNROLL = 4


def _pipelined_key_blocks(qi, nd, tk, scores, consume, finish, init):
    n_full = qi * nd
    scores(0, 0, None)

    def group(g, carry):
        j = KEY_BLOCK_UNROLL * g
        for u in range(KEY_BLOCK_UNROLL):
            scores(j + u + 1, (u + 1) % 2, None)
            carry = consume(u % 2, j + u, carry, None)
        return carry

    carry = lax.fori_loop(0, n_full // KEY_BLOCK_UNROLL, group, init)
    base = (n_full // KEY_BLOCK_UNROLL) * KEY_BLOCK_UNROLL

    for rem in range(0, KEY_BLOCK_UNROLL, math.gcd(nd, KEY_BLOCK_UNROLL)):
        @pl.when(n_full % KEY_BLOCK_UNROLL == rem)
        def _(rem=rem):
            blocks = [(base + u, None) for u in range(rem)] + [(n_full + d, d * tk) for d in range(nd)]
            c = carry
            for idx, (j, q_lo) in enumerate(blocks):
                if idx + 1 < len(blocks):
                    scores(blocks[idx + 1][0], (idx + 1) % 2, blocks[idx + 1][1])
                c = consume(idx % 2, j, c, q_lo)
            finish(c)


def _mla_attn_kernel(qT_ref, k_ref, vT_ref, o_ref, sa_ref, sb_ref, acc_ref):
    tq = qT_ref.shape[4]
    tk = vT_ref.shape[4]
    qi = pl.program_id(2)
    qT = qT_ref[0, 0, 0]
    s_refs = (sa_ref, sb_ref)

    def scores(j, slot, q_lo):
        off = pl.multiple_of(j * tk, tk)
        lo = q_lo or 0
        s_refs[slot][:, lo:] = _dot(k_ref[0, 0, pl.ds(off, tk), :], qT[:, lo:])

    def consume(slot, j, carry, q_lo):
        return _fold_key_block(s_refs[slot], acc_ref, vT_ref[0, 0, j], carry, q_lo)

    def finish(carry):
        l = acc_ref[MLA_V:MLA_V + 1, :]
        o_ref[0] = (acc_ref[0:MLA_V, :] / l).T.astype(o_ref.dtype)

    acc_ref[...] = jnp.zeros(acc_ref.shape, F32)
    init = jnp.full((1, tq), NEG_BIG, F32)
    _pipelined_key_blocks(qi, tq // tk, tk, scores, consume, finish, init)


def _mla_attn(qT, k, vT):
    b, nh, nq, _, tq = qT.shape
    nkb, tk = vT.shape[2], vT.shape[4]
    s = nq * tq
    return pl.pallas_call(
        _mla_attn_kernel,
        grid=(b, nh, nq),
        in_specs=[
            pl.BlockSpec((1, 1, 1, MLA_DK, tq), lambda i, h, t: (i, h, t, 0, 0)),
            pl.BlockSpec((1, 1, s, MLA_DK), lambda i, h, t: (i, h, 0, 0)),
            pl.BlockSpec((1, 1, nkb, MLA_V + SUM_ROWS, tk), lambda i, h, t: (i, h, 0, 0, 0)),
        ],
        out_specs=pl.BlockSpec((1, tq, MLA_V), lambda i, h, t: (i, t, h)),
        out_shape=jax.ShapeDtypeStruct((b, s, nh * MLA_V), BF16),
        scratch_shapes=[pltpu.VMEM((tk, tq), F32), pltpu.VMEM((tk, tq), F32),
                        pltpu.VMEM((MLA_V + SUM_ROWS, tq), F32)],
        compiler_params=_params(3),
        name="mla_attn",
    )(qT, k, vT)


def _diff_attn_kernel(lam_init, q1T_ref, q2T_ref, k1_ref, k2_ref, vT_ref, lam_ref, sub_ref, o_ref,
                      s1a_ref, s1b_ref, s2a_ref, s2b_ref, acc1_ref, acc2_ref):
    tq = q1T_ref.shape[4]
    tk = vT_ref.shape[4]
    qi = pl.program_id(2)
    q1T = q1T_ref[0, 0, 0]
    q2T = q2T_ref[0, 0, 0]
    s1_refs = (s1a_ref, s1b_ref)
    s2_refs = (s2a_ref, s2b_ref)

    def scores(j, slot, q_lo):
        off = pl.multiple_of(j * tk, tk)
        lo = q_lo or 0
        s1_refs[slot][:, lo:] = _dot(k1_ref[0, 0, pl.ds(off, tk), :], q1T[:, lo:])
        s2_refs[slot][:, lo:] = _dot(k2_ref[0, 0, pl.ds(off, tk), :], q2T[:, lo:])

    def consume(slot, j, carry, q_lo):
        vT = vT_ref[0, 0, j]
        m1 = _fold_key_block(s1_refs[slot], acc1_ref, vT, carry[0], q_lo)
        m2 = _fold_key_block(s2_refs[slot], acc2_ref, vT, carry[1], q_lo)
        return m1, m2

    def finish(carry):
        l1 = acc1_ref[DIFF_V:DIFF_V + 1, :]
        l2 = acc2_ref[DIFF_V:DIFF_V + 1, :]
        lp = lam_ref[...]
        lam = (jnp.exp(jnp.sum(lp[0:1] * lp[1:2], axis=-1, keepdims=True))
               - jnp.exp(jnp.sum(lp[2:3] * lp[3:4], axis=-1, keepdims=True)) + lam_init)
        oT = acc1_ref[0:DIFF_V, :] / l1 - lam * (acc2_ref[0:DIFF_V, :] / l2)
        o_ref[0] = (_rms(oT.T, sub_ref[...]) * (1.0 - lam_init)).astype(o_ref.dtype)

    acc1_ref[...] = jnp.zeros(acc1_ref.shape, F32)
    acc2_ref[...] = jnp.zeros(acc2_ref.shape, F32)
    m_init = jnp.full((1, tq), NEG_BIG, F32)
    _pipelined_key_blocks(qi, tq // tk, tk, scores, consume, finish, (m_init, m_init))


def _diff_attn(q1T, q2T, k1, k2, vT, lam_params, subln, lam_init):
    b, nh, nq, _, tq = q1T.shape
    nkb, tk = vT.shape[2], vT.shape[4]
    s = nq * tq
    qT_spec = pl.BlockSpec((1, 1, 1, DIFF_DIM, tq), lambda i, h, t: (i, h, t, 0, 0))
    k_spec = pl.BlockSpec((1, 1, s, DIFF_DIM), lambda i, h, t: (i, h, 0, 0))
    return pl.pallas_call(
        functools.partial(_diff_attn_kernel, lam_init),
        grid=(b, nh, nq),
        in_specs=[
            qT_spec, qT_spec, k_spec, k_spec,
            pl.BlockSpec((1, 1, nkb, DIFF_V + SUM_ROWS, tk), lambda i, h, t: (i, h, 0, 0, 0)),
            _const_spec(lam_params.shape),
            _const_spec(subln.shape),
        ],
        out_specs=pl.BlockSpec((1, tq, DIFF_V), lambda i, h, t: (i, t, h)),
        out_shape=jax.ShapeDtypeStruct((b, s, nh * DIFF_V), BF16),
        scratch_shapes=[pltpu.VMEM((tk, tq), F32)] * 4 + [pltpu.VMEM((DIFF_V + SUM_ROWS, tq), F32)] * 2,
        compiler_params=_params(3),
        name="diff_attn",
    )(q1T, q2T, k1, k2, vT, lam_params, subln)


def _out_proj_kernel(o_ref, om_ref, x_ref, w1_ref, w2_ref, g_ref, out_ref):
    mix = _dot(o_ref[0], w1_ref[...]) + _dot(om_ref[0], w2_ref[...])
    out_ref[0] = x_ref[0] + _rms(mix, g_ref[...])


def _out_proj(o, om, x, w1, w2, gain, tm):
    b, s, d = x.shape
    return pl.pallas_call(
        _out_proj_kernel,
        grid=(b, s // tm),
        in_specs=[
            pl.BlockSpec((1, tm, o.shape[-1]), lambda i, t: (i, t, 0)),
            pl.BlockSpec((1, tm, om.shape[-1]), lambda i, t: (i, t, 0)),
            pl.BlockSpec((1, tm, d), lambda i, t: (i, t, 0)),
            _const_spec(w1.shape),
            _const_spec(w2.shape),
            _const_spec(gain.shape),
        ],
        out_specs=pl.BlockSpec((1, tm, d), lambda i, t: (i, t, 0)),
        out_shape=jax.ShapeDtypeStruct((b, s, d), F32),
        compiler_params=_params(2),
        name="out_proj",
    )(o, om, x, w1, w2, gain)


FFN_SUB = 256
FFN_ROWS = 256
NORM_ROWS = 32


def _gelu_tanh(g):
    c = math.sqrt(2.0 / math.pi)
    return 0.5 * g * (1.0 + jnp.tanh(c * (g + 0.044715 * (g * g * g))))


def _causal_conv(u_ref, r, cw, cb):
    base = SUBLANES + r
    u0 = u_ref[base:base + FFN_ROWS, :]
    u1 = u_ref[base - 1:base - 1 + FFN_ROWS, :]
    u2 = u_ref[base - 2:base - 2 + FFN_ROWS, :]
    return cb + cw[0:1] * u2 + cw[1:2] * u1 + cw[2:3] * u0


def _conv_ffn_kernel(x_ref, pre_ref, wg_ref, wv_ref, cwg_ref, cwv_ref, cbg_ref, cbv_ref, wo_ref,
                     post_ref, out_ref, h_ref, acc_ref, tail_ref, u_ref):
    t = pl.program_id(1)
    c = pl.program_id(2)
    nc = pl.num_programs(2)
    tm = x_ref.shape[1]
    fc = wg_ref.shape[1]

    @pl.when(c == 0)
    def _():
        h_ref[...] = _rms(x_ref[0], pre_ref[...]).astype(BF16)
        acc_ref[...] = jnp.zeros(acc_ref.shape, F32)

    @pl.when(t == 0)
    def _():
        tail_ref[c] = jnp.zeros(tail_ref.shape[1:], F32)

    subs = list(range(0, fc, FFN_SUB))
    for si, lo in enumerate(subs):
        u_ref[si, 0, 0:SUBLANES, :] = tail_ref[c, 0, :, lo:lo + FFN_SUB]
        u_ref[si, 1, 0:SUBLANES, :] = tail_ref[c, 1, :, lo:lo + FFN_SUB]

    groups = [(si, r) for si in range(len(subs)) for r in range(0, tm, FFN_ROWS)]

    def up(si, r):
        lo = subs[si]
        hr = h_ref[r:r + FFN_ROWS, :]
        rows = slice(SUBLANES + r, SUBLANES + r + FFN_ROWS)
        u_ref[si, 0, rows, :] = _dot(hr, wg_ref[:, lo:lo + FFN_SUB])
        u_ref[si, 1, rows, :] = _dot(hr, wv_ref[:, lo:lo + FFN_SUB])

    for g in groups:
        up(*g)
    ys = []
    for si, r in groups:
        lo, hi = subs[si], subs[si] + FFN_SUB
        cg = _causal_conv(u_ref.at[si, 0], r, cwg_ref[:, lo:hi], cbg_ref[:, lo:hi])
        cv = _causal_conv(u_ref.at[si, 1], r, cwv_ref[:, lo:hi], cbv_ref[:, lo:hi])
        ys.append((_gelu_tanh(cg) * cv).astype(BF16))
    for (si, r), y in zip(groups, ys):
        lo, hi = subs[si], subs[si] + FFN_SUB
        acc_ref[r:r + FFN_ROWS, :] += _dot(y, wo_ref[lo:hi, :])

    for si, lo in enumerate(subs):
        tail_ref[c, 0, :, lo:lo + FFN_SUB] = u_ref[si, 0, tm:tm + SUBLANES, :]
        tail_ref[c, 1, :, lo:lo + FFN_SUB] = u_ref[si, 1, tm:tm + SUBLANES, :]

    @pl.when(c == nc - 1)
    def _():
        for r in range(0, tm, NORM_ROWS):
            rows = slice(r, r + NORM_ROWS)
            out_ref[0, rows, :] = x_ref[0, rows, :] + _rms(acc_ref[rows, :], post_ref[...])


def _conv_ffn(x, pre, w_in, conv_w, conv_b, w_out, post, tm, fc):
    b, s, d = x.shape
    nc = D_FF // fc
    return pl.pallas_call(
        _conv_ffn_kernel,
        grid=(b, s // tm, nc),
        in_specs=[
            pl.BlockSpec((1, tm, d), lambda i, t, c: (i, t, 0)),
            _const_spec(pre.shape),
            pl.BlockSpec((d, fc), lambda i, t, c: (0, c)),
            pl.BlockSpec((d, fc), lambda i, t, c: (0, c + nc)),
            pl.BlockSpec((CONV_WIDTH, fc), lambda i, t, c: (0, c)),
            pl.BlockSpec((CONV_WIDTH, fc), lambda i, t, c: (0, c + nc)),
            pl.BlockSpec((1, fc), lambda i, t, c: (0, c)),
            pl.BlockSpec((1, fc), lambda i, t, c: (0, c + nc)),
            pl.BlockSpec((fc, d), lambda i, t, c: (c, 0)),
            _const_spec(post.shape),
        ],
        out_specs=pl.BlockSpec((1, tm, d), lambda i, t, c: (i, t, 0)),
        out_shape=jax.ShapeDtypeStruct((b, s, d), F32),
        scratch_shapes=[
            pltpu.VMEM((tm, d), BF16),
            pltpu.VMEM((tm, d), F32),
            pltpu.VMEM((nc, 2, SUBLANES, fc), F32),
            pltpu.VMEM((fc // FFN_SUB, 2, SUBLANES + tm, FFN_SUB), F32),
        ],
        compiler_params=_params(3),
        name="conv_ffn",
    )(x, pre, w_in, w_in, conv_w, conv_w, conv_b, conv_b, w_out, post)


def _b_proj_kernel(x_ref, pos_ref, gs_ref, gb_ref, w_k_ref, w_vT_ref, w_qT_ref, w_qm_ref,
                   mkT_ref, mv_ref, q1T_ref, q2T_ref, k1_ref, k2_ref, vT_ref, om_ref):
    xf = x_ref[0]
    n = xf * lax.rsqrt(jnp.mean(xf * xf, axis=-1, keepdims=True) + NORM_EPS)
    hs = (n * gs_ref[...]).astype(BF16)
    hb = (n * gb_ref[...]).astype(BF16)

    half = DIFF_DIM // 2
    cos_r, sin_r = _rope_tables_rows(pos_ref[0], half, DIFF_DIM)

    tab_c = jnp.concatenate([cos_r, cos_r], axis=0).T
    tab_s = jnp.concatenate([-sin_r, sin_r], axis=0).T
    kk = _dot(hs, w_k_ref[...])
    vT = _dot_nt(w_vT_ref[...], hs)
    nk = DIFF_HEADS * DIFF_DIM
    for hd in range(DIFF_HEADS):
        for which, ref in ((0, k1_ref), (1, k2_ref)):
            base = which * nk + hd * DIFF_DIM
            xr = kk[:, base:base + DIFF_DIM]
            ref[0, hd] = (xr * tab_c + pltpu.roll(xr, half, 1) * tab_s).astype(BF16)
        vT_ref[0, hd, 0, 0:DIFF_V, :] = vT[hd * DIFF_V:(hd + 1) * DIFF_V].astype(BF16)
        vT_ref[0, hd, 0, DIFF_V:DIFF_V + SUM_ROWS, :] = jnp.ones((SUM_ROWS, x_ref.shape[1]), BF16)

    scale = DIFF_DIM ** -0.5 * LOG2_E
    cos_q, sin_q = cos_r * scale, sin_r * scale
    qT = _dot_nt(w_qT_ref[...], hb)
    for hd in range(DIFF_HEADS):
        for which, ref in ((0, q1T_ref), (1, q2T_ref)):
            base = hd * 2 * DIFF_DIM + which * DIFF_DIM
            r1, r2 = _rope_rows(qT[base:base + half], qT[base + half:base + DIFF_DIM], cos_q, sin_q)
            ref[0, hd, 0] = jnp.concatenate([r1, r2], axis=0).astype(BF16)

    _mem_attention(_dot(hb, w_qm_ref[...]), mkT_ref, mv_ref, om_ref)


def _b_proj(x, pos_row, gs, gb, w_k, w_vT, w_qT, w_qm, mkT, mv, tm, tq):
    b, s, d = x.shape
    nt = s // tm
    r = tq // tm
    m = mkT.shape[-1]
    qT_shape = jax.ShapeDtypeStruct((b, DIFF_HEADS, s // tq, DIFF_DIM, tq), BF16)
    k_shape = jax.ShapeDtypeStruct((b, DIFF_HEADS, s, DIFF_DIM), BF16)
    qT_spec = pl.BlockSpec((1, DIFF_HEADS, 1, DIFF_DIM, tm), lambda i, t: (i, 0, t // r, 0, t % r))
    k_spec = pl.BlockSpec((1, DIFF_HEADS, tm, DIFF_DIM), lambda i, t: (i, 0, t, 0))
    return pl.pallas_call(
        _b_proj_kernel,
        grid=(b, nt),
        in_specs=[
            pl.BlockSpec((1, tm, d), lambda i, t: (i, t, 0)),
            pl.BlockSpec((1, 1, tm), lambda i, t: (i, 0, t)),
            _const_spec(gs.shape),
            _const_spec(gb.shape),
            _const_spec(w_k.shape),
            _const_spec(w_vT.shape),
            _const_spec(w_qT.shape),
            _const_spec(w_qm.shape),
            pl.BlockSpec((1, 1, MEM_HEADS, MEM_DIM, m), lambda i, t: (1, i, 0, 0, 0)),
            pl.BlockSpec((1, 1, MEM_HEADS, m, MEM_DIM), lambda i, t: (1, i, 0, 0, 0)),
        ],
        out_specs=[
            qT_spec, qT_spec, k_spec, k_spec,
            pl.BlockSpec((1, DIFF_HEADS, 1, DIFF_V + SUM_ROWS, tm), lambda i, t: (i, 0, t, 0, 0)),
            pl.BlockSpec((1, tm, MEM_HEADS * MEM_DIM), lambda i, t: (i, t, 0)),
        ],
        out_shape=[
            qT_shape, qT_shape, k_shape, k_shape,
            jax.ShapeDtypeStruct((b, DIFF_HEADS, nt, DIFF_V + SUM_ROWS, tm), BF16),
            jax.ShapeDtypeStruct((b, s, MEM_HEADS * MEM_DIM), BF16),
        ],
        compiler_params=_params(2),
        name="b_proj",
    )(x, pos_row, gs, gb, w_k, w_vT, w_qT, w_qm, mkT, mv)


def _row(v):
    return v.reshape(1, -1).astype(F32)


def kernel(x, mem, positions, a_attn_norm_pre, a_w_in, a_q_norm, a_w_uq, a_kv_norm, a_w_ukv, a_mem_norm, a_w_mem_kv, a_w_o, a_attn_norm_post, shared_kv_norm, shared_w_kv, b_attn_norm_pre, b_w_in, b_lambda_q1, b_lambda_k1, b_lambda_q2, b_lambda_k2, b_subln, b_mem_norm, b_w_mem_kv, b_w_o, b_attn_norm_post, ffn_norm_pre, ffn_w_in, ffn_conv_w, ffn_conv_b, ffn_w_out, ffn_norm_post):
    b, s, d = x.shape
    assert d == D_MODEL and a_attn_norm_pre.shape[0] == 1 and b_attn_norm_pre.shape[0] == 1
    tm, tq_mla, tq_diff, fc = _tiles(s)
    assert all(s % tq == 0 and tq % tm == 0 for tq in (tq_mla, tq_diff)) and D_FF % fc == 0
    pos_row = positions.reshape(b, 1, s)

    mem_gains = jnp.stack([a_mem_norm[0], b_mem_norm[0]]).reshape(2, 1, d)
    mem_w = jnp.stack([a_w_mem_kv[0], b_w_mem_kv[0]]).astype(BF16)
    mkT, mv = _mem_kv(mem, mem_gains, mem_w)

    w_in = a_w_in[0]
    i2 = Q_LORA + KV_LORA + MLA_ROPE
    w_in_p = jnp.concatenate(
        [w_in[:, :i2], jnp.zeros((d, LANES - MLA_ROPE), F32), w_in[:, i2:]], axis=1).astype(BF16)
    w_uqT = a_w_uq[0].T.astype(BF16)
    w_ukv = a_w_ukv[0].reshape(KV_LORA, MLA_HEADS, MLA_NOPE + MLA_V)
    w_k = w_ukv[:, :, :MLA_NOPE].reshape(KV_LORA, MLA_HEADS * MLA_NOPE).astype(BF16)
    w_vT = w_ukv[:, :, MLA_NOPE:].reshape(KV_LORA, MLA_HEADS * MLA_V).T.astype(BF16)
    qT, k, vT, om = _a_proj(x, pos_row, _row(a_attn_norm_pre[0]), w_in_p, _row(a_q_norm[0]), w_uqT,
                            _row(a_kv_norm[0]), w_k, w_vT, mkT, mv, tm, tq_mla)
    o = _mla_attn(qT, k, vT)
    n_main = MLA_HEADS * MLA_V
    w_o = a_w_o[0].astype(BF16)
    x = _out_proj(o, om, x, w_o[:n_main], w_o[n_main:], _row(a_attn_norm_post[0]), tm)
    x = _conv_ffn(x, _row(ffn_norm_pre[0]), ffn_w_in[0].astype(BF16), ffn_conv_w[0],
                  _row(ffn_conv_b[0]), ffn_w_out[0].astype(BF16), _row(ffn_norm_post[0]), tm, fc)

    layer = N_A_LAYERS
    lam_init = 0.8 - 0.6 * math.exp(-0.3 * layer)
    w_kv = shared_w_kv.reshape(d, DIFF_HEADS, 2 * DIFF_DIM + DIFF_V)
    nk = DIFF_HEADS * DIFF_DIM
    w_k12 = jnp.concatenate(
        [w_kv[:, :, :DIFF_DIM].reshape(d, nk), w_kv[:, :, DIFF_DIM:2 * DIFF_DIM].reshape(d, nk)],
        axis=1).astype(BF16)
    w_svT = w_kv[:, :, 2 * DIFF_DIM:].reshape(d, DIFF_HEADS * DIFF_V).T.astype(BF16)
    nq_cols = DIFF_HEADS * 2 * DIFF_DIM
    w_bq_T = b_w_in[0][:, :nq_cols].T.astype(BF16)
    w_bqm = b_w_in[0][:, nq_cols:].astype(BF16)
    q1T, q2T, k1, k2, vbT, omb = _b_proj(x, pos_row, _row(shared_kv_norm), _row(b_attn_norm_pre[0]),
                                         w_k12, w_svT, w_bq_T, w_bqm, mkT, mv, tm, tq_diff)
    lam_params = jnp.stack([b_lambda_q1[0], b_lambda_k1[0], b_lambda_q2[0], b_lambda_k2[0]]).astype(F32)
    ob = _diff_attn(q1T, q2T, k1, k2, vbT, lam_params, _row(b_subln[0]), lam_init)
    n_main_b = DIFF_HEADS * DIFF_V
    w_ob = b_w_o[0].astype(BF16)
    x = _out_proj(ob, omb, x, w_ob[:n_main_b], w_ob[n_main_b:], _row(b_attn_norm_post[0]), tm)
    x = _conv_ffn(x, _row(ffn_norm_pre[1]), ffn_w_in[1].astype(BF16), ffn_conv_w[1],
                  _row(ffn_conv_b[1]), ffn_w_out[1].astype(BF16), _row(ffn_norm_post[1]), tm, fc)
    return x
```

```python
import functools
import math

import jax
import jax.numpy as jnp
from jax import lax
from jax.experimental import pallas as pl
from jax.experimental.pallas import tpu as pltpu

D_MODEL = 2048
ROPE_THETA = 10000.0
NORM_EPS = 1e-6
MLA_HEADS = 12
MLA_NOPE = 128
MLA_ROPE = 64
MLA_V = 128
MLA_DK = MLA_NOPE + MLA_ROPE
Q_LORA = 512
KV_LORA = 512
DIFF_HEADS = 6
DIFF_DIM = 128
DIFF_V = 2 * DIFF_DIM
MEM_HEADS = 4
MEM_DIM = 128
D_FF = 5632
CONV_WIDTH = 3
N_A_LAYERS = 1

LANES = 128
SUBLANES = 8
SUM_ROWS = 16
A_IN_PAD = Q_LORA + KV_LORA + LANES + MEM_HEADS * MEM_DIM

V7X_VMEM_BYTES = 64 * 1024 * 1024
VMEM_LIMIT = V7X_VMEM_BYTES - 8 * 1024 * 1024

NEG_BIG = -1e30
LOG2_E = math.log2(math.e)
BF16 = jnp.bfloat16
F32 = jnp.float32


def _tiles(seq):
    tm = min(512, seq)
    tq_mla = min(1024, seq)
    tq_diff = min(1024, seq)
    fc = 512
    return tm, tq_mla, tq_diff, fc


def _params(n_axes):
    return pltpu.CompilerParams(dimension_semantics=("arbitrary",) * n_axes,
                                vmem_limit_bytes=VMEM_LIMIT)


def _const_spec(shape):
    nd = len(shape)
    return pl.BlockSpec(shape, lambda *_: (0,) * nd, pipeline_mode=pl.Buffered(1))


def _rms(xf, gain):
    ms = jnp.mean(xf * xf, axis=-1, keepdims=True)
    return xf * lax.rsqrt(ms + NORM_EPS) * gain


def _dot(a, b):
    return jnp.dot(a, b, preferred_element_type=F32)


def _dot_nt(a, b):
    return lax.dot_general(a, b, (((1,), (1,)), ((), ())), preferred_element_type=F32)


def _rope_tables_rows(pos_row, half, dh):
    tokens = pos_row.shape[-1]
    j = lax.broadcasted_iota(jnp.int32, (half, tokens), 0).astype(F32)
    inv_freq = jnp.exp(j * (-math.log(ROPE_THETA) * 2.0 / dh))
    ang = pos_row.astype(F32) * inv_freq
    return jnp.cos(ang), jnp.sin(ang)


def _rope_rows(a, b, cos_r, sin_r):
    return a * cos_r - b * sin_r, b * cos_r + a * sin_r


def _mem_attention(qm, mkT_ref, mv_ref, out_ref):
    scale = MEM_DIM ** -0.5
    for h in range(MEM_HEADS):
        qh = (qm[:, h * MEM_DIM:(h + 1) * MEM_DIM] * scale).astype(BF16)
        s = _dot(qh, mkT_ref[0, 0, h])
        m = jnp.max(s, axis=-1, keepdims=True)
        p = jnp.exp(s - m)
        l = jnp.sum(p, axis=-1, keepdims=True)
        o = _dot(p.astype(BF16), mv_ref[0, 0, h])
        out_ref[0, :, h * MEM_DIM:(h + 1) * MEM_DIM] = (o / l).astype(out_ref.dtype)


def _mem_kv_kernel(mem_ref, g_ref, w_ref, kT_ref, v_ref):
    n = _rms(mem_ref[0], g_ref[0]).astype(BF16)
    kv = _dot(n, w_ref[0])
    for h in range(MEM_HEADS):
        base = h * 2 * MEM_DIM
        kT_ref[0, 0, h] = kv[:, base:base + MEM_DIM].T.astype(BF16)
        v_ref[0, 0, h] = kv[:, base + MEM_DIM:base + 2 * MEM_DIM].astype(BF16)


def _mem_kv(mem, gains, weights):
    b, m, d = mem.shape
    nl = gains.shape[0]
    return pl.pallas_call(
        _mem_kv_kernel,
        grid=(nl, b),
        in_specs=[
            pl.BlockSpec((1, m, d), lambda l, i: (i, 0, 0)),
            pl.BlockSpec((1, 1, d), lambda l, i: (l, 0, 0)),
            pl.BlockSpec((1, d, MEM_HEADS * 2 * MEM_DIM), lambda l, i: (l, 0, 0)),
        ],
        out_specs=[
            pl.BlockSpec((1, 1, MEM_HEADS, MEM_DIM, m), lambda l, i: (l, i, 0, 0, 0)),
            pl.BlockSpec((1, 1, MEM_HEADS, m, MEM_DIM), lambda l, i: (l, i, 0, 0, 0)),
        ],
        out_shape=[
            jax.ShapeDtypeStruct((nl, b, MEM_HEADS, MEM_DIM, m), BF16),
            jax.ShapeDtypeStruct((nl, b, MEM_HEADS, m, MEM_DIM), BF16),
        ],
        compiler_params=_params(2),
        name="mem_kv",
    )(mem, gains, weights)


def _a_proj_kernel(x_ref, pos_ref, g_ref, w_in_ref, qg_ref, w_uqT_ref, kvg_ref, w_k_ref, w_vT_ref,
                   mkT_ref, mv_ref, qT_ref, k_ref, vT_ref, om_ref):
    tm = x_ref.shape[1]
    h = _rms(x_ref[0], g_ref[...]).astype(BF16)
    proj = _dot(h, w_in_ref[...])
    ql = _rms(proj[:, :Q_LORA], qg_ref[...]).astype(BF16)
    kvl = _rms(proj[:, Q_LORA:Q_LORA + KV_LORA], kvg_ref[...]).astype(BF16)
    pe = proj[:, Q_LORA + KV_LORA:Q_LORA + KV_LORA + LANES]
    qm = proj[:, Q_LORA + KV_LORA + LANES:]

    half = MLA_ROPE // 2
    cos_r, sin_r = _rope_tables_rows(pos_ref[0], half, MLA_ROPE)

    peT = pe.T
    r1, r2 = _rope_rows(peT[0:half], peT[half:2 * half], cos_r, sin_r)
    kpe = jnp.concatenate([r1, r2, jnp.zeros((LANES - MLA_ROPE, tm), F32)], axis=0).T
    kpe = kpe[:, :MLA_ROPE].astype(BF16)

    scale = MLA_DK ** -0.5 * LOG2_E
    cos_q, sin_q = cos_r * scale, sin_r * scale
    qT = _dot_nt(w_uqT_ref[...], ql)
    kn = _dot(kvl, w_k_ref[...])
    vT = _dot_nt(w_vT_ref[...], kvl)
    for hd in range(MLA_HEADS):
        base = hd * MLA_DK
        qT_ref[0, hd, 0, 0:MLA_NOPE, :] = (qT[base:base + MLA_NOPE] * scale).astype(BF16)
        r1, r2 = _rope_rows(qT[base + MLA_NOPE:base + MLA_NOPE + half],
                            qT[base + MLA_NOPE + half:base + MLA_DK], cos_q, sin_q)
        qT_ref[0, hd, 0, MLA_NOPE:MLA_NOPE + half, :] = r1.astype(BF16)
        qT_ref[0, hd, 0, MLA_NOPE + half:MLA_DK, :] = r2.astype(BF16)
        k_ref[0, hd, :, 0:MLA_NOPE] = kn[:, hd * MLA_NOPE:(hd + 1) * MLA_NOPE].astype(BF16)
        k_ref[0, hd, :, MLA_NOPE:MLA_DK] = kpe
        vT_ref[0, hd, 0, 0:MLA_V, :] = vT[hd * MLA_V:(hd + 1) * MLA_V].astype(BF16)
        vT_ref[0, hd, 0, MLA_V:MLA_V + SUM_ROWS, :] = jnp.ones((SUM_ROWS, tm), BF16)

    _mem_attention(qm, mkT_ref, mv_ref, om_ref)


def _a_proj(x, pos_row, gain, w_in, qg, w_uqT, kvg, w_k, w_vT, mkT, mv, tm, tq):
    b, s, d = x.shape
    nt = s // tm
    r = tq // tm
    m = mkT.shape[-1]
    return pl.pallas_call(
        _a_proj_kernel,
        grid=(b, nt),
        in_specs=[
            pl.BlockSpec((1, tm, d), lambda i, t: (i, t, 0)),
            pl.BlockSpec((1, 1, tm), lambda i, t: (i, 0, t)),
            _const_spec(gain.shape),
            _const_spec(w_in.shape),
            _const_spec(qg.shape),
            _const_spec(w_uqT.shape),
            _const_spec(kvg.shape),
            _const_spec(w_k.shape),
            _const_spec(w_vT.shape),
            pl.BlockSpec((1, 1, MEM_HEADS, MEM_DIM, m), lambda i, t: (0, i, 0, 0, 0)),
            pl.BlockSpec((1, 1, MEM_HEADS, m, MEM_DIM), lambda i, t: (0, i, 0, 0, 0)),
        ],
        out_specs=[
            pl.BlockSpec((1, MLA_HEADS, 1, MLA_DK, tm), lambda i, t: (i, 0, t // r, 0, t % r)),
            pl.BlockSpec((1, MLA_HEADS, tm, MLA_DK), lambda i, t: (i, 0, t, 0)),
            pl.BlockSpec((1, MLA_HEADS, 1, MLA_V + SUM_ROWS, tm), lambda i, t: (i, 0, t, 0, 0)),
            pl.BlockSpec((1, tm, MEM_HEADS * MEM_DIM), lambda i, t: (i, t, 0)),
        ],
        out_shape=[
            jax.ShapeDtypeStruct((b, MLA_HEADS, s // tq, MLA_DK, tq), BF16),
            jax.ShapeDtypeStruct((b, MLA_HEADS, s, MLA_DK), BF16),
            jax.ShapeDtypeStruct((b, MLA_HEADS, nt, MLA_V + SUM_ROWS, tm), BF16),
            jax.ShapeDtypeStruct((b, s, MEM_HEADS * MEM_DIM), BF16),
        ],
        compiler_params=_params(2),
        name="a_proj",
    )(x, pos_row, gain, w_in, qg, w_uqT, kvg, w_k, w_vT, mkT, mv)


def _online_softmax_pv(s_ref, acc_ref, vT, cols, m_prev, masked):
    def read():
        s = s_ref[:, cols]
        if masked:
            key = lax.broadcasted_iota(jnp.int32, s.shape, 0)
            qry = lax.broadcasted_iota(jnp.int32, s.shape, 1)
            s = jnp.where(key <= qry, s, NEG_BIG)
        return s

    m_new = jnp.maximum(m_prev, jnp.max(read(), axis=0, keepdims=True))
    acc_ref[:, cols] = jnp.exp2(m_prev - m_new) * acc_ref[:, cols]
    p = jnp.exp2(read() - m_new).astype(BF16)
    acc_ref[:, cols] += _dot(vT, p)
    return m_new


def _fold_key_block(s_ref, acc_ref, vT, m_prev, q_lo):
    if q_lo is None:
        qc = min(QUERY_CHUNK, s_ref.shape[1])
        parts = []
        for c0 in range(0, s_ref.shape[1], qc):
            cols = slice(c0, c0 + qc)
            parts.append(_online_softmax_pv(s_ref, acc_ref, vT, cols, m_prev[:, cols], False))
        return jnp.concatenate(parts, axis=1)
    m_part = _online_softmax_pv(s_ref, acc_ref, vT, slice(q_lo, None), m_prev[:, q_lo:], True)
    return m_part if q_lo == 0 else jnp.concatenate([m_prev[:, :q_lo], m_part], axis=1)


QUERY_CHUNK = 512
KEY_BLOCK_UNROLL = 4


def _pipelined_key_blocks(qi, nd, tk, scores, consume, finish, init):
    n_full = qi * nd
    scores(0, 0, None)

    def group(g, carry):
        j = KEY_BLOCK_UNROLL * g
        for u in range(KEY_BLOCK_UNROLL):
            scores(j + u + 1, (u + 1) % 2, None)
            carry = consume(u % 2, j + u, carry, None)
        return carry

    carry = lax.fori_loop(0, n_full // KEY_BLOCK_UNROLL, group, init)
    base = (n_full // KEY_BLOCK_UNROLL) * KEY_BLOCK_UNROLL

    for rem in range(0, KEY_BLOCK_UNROLL, math.gcd(nd, KEY_BLOCK_UNROLL)):
        @pl.when(n_full % KEY_BLOCK_UNROLL == rem)
        def _(rem=rem):
            blocks = [(base + u, None) for u in range(rem)] + [(n_full + d, d * tk) for d in range(nd)]
            c = carry
            for idx, (j, q_lo) in enumerate(blocks):
                if idx + 1 < len(blocks):
                    scores(blocks[idx + 1][0], (idx + 1) % 2, blocks[idx + 1][1])
                c = consume(idx % 2, j, c, q_lo)
            finish(c)


def _mla_attn_kernel(qT_ref, k_ref, vT_ref, o_ref, sa_ref, sb_ref, acc_ref):
    tq = qT_ref.shape[4]
    tk = vT_ref.shape[4]
    qi = pl.program_id(2)
    qT = qT_ref[0, 0, 0]
    s_refs = (sa_ref, sb_ref)

    def scores(j, slot, q_lo):
        off = pl.multiple_of(j * tk, tk)
        lo = q_lo or 0
        s_refs[slot][:, lo:] = _dot(k_ref[0, 0, pl.ds(off, tk), :], qT[:, lo:])

    def consume(slot, j, carry, q_lo):
        return _fold_key_block(s_refs[slot], acc_ref, vT_ref[0, 0, j], carry, q_lo)

    def finish(carry):
        l = acc_ref[MLA_V:MLA_V + 1, :]
        o_ref[0] = (acc_ref[0:MLA_V, :] / l).T.astype(o_ref.dtype)

    acc_ref[...] = jnp.zeros(acc_ref.shape, F32)
    init = jnp.full((1, tq), NEG_BIG, F32)
    _pipelined_key_blocks(qi, tq // tk, tk, scores, consume, finish, init)


def _mla_attn(qT, k, vT):
    b, nh, nq, _, tq = qT.shape
    nkb, tk = vT.shape[2], vT.shape[4]
    s = nq * tq
    return pl.pallas_call(
        _mla_attn_kernel,
        grid=(b, nh, nq),
        in_specs=[
            pl.BlockSpec((1, 1, 1, MLA_DK, tq), lambda i, h, t: (i, h, t, 0, 0)),
            pl.BlockSpec((1, 1, s, MLA_DK), lambda i, h, t: (i, h, 0, 0)),
            pl.BlockSpec((1, 1, nkb, MLA_V + SUM_ROWS, tk), lambda i, h, t: (i, h, 0, 0, 0)),
        ],
        out_specs=pl.BlockSpec((1, tq, MLA_V), lambda i, h, t: (i, t, h)),
        out_shape=jax.ShapeDtypeStruct((b, s, nh * MLA_V), BF16),
        scratch_shapes=[pltpu.VMEM((tk, tq), F32), pltpu.VMEM((tk, tq), F32),
                        pltpu.VMEM((MLA_V + SUM_ROWS, tq), F32)],
        compiler_params=_params(3),
        name="mla_attn",
    )(qT, k, vT)


def _diff_attn_kernel(lam_init, q1T_ref, q2T_ref, k1_ref, k2_ref, vT_ref, lam_ref, sub_ref, o_ref,
                      s1a_ref, s1b_ref, s2a_ref, s2b_ref, acc1_ref, acc2_ref):
    tq = q1T_ref.shape[4]
    tk = vT_ref.shape[4]
    qi = pl.program_id(2)
    q1T = q1T_ref[0, 0, 0]
    q2T = q2T_ref[0, 0, 0]
    s1_refs = (s1a_ref, s1b_ref)
    s2_refs = (s2a_ref, s2b_ref)

    def scores(j, slot, q_lo):
        off = pl.multiple_of(j * tk, tk)
        lo = q_lo or 0
        s1_refs[slot][:, lo:] = _dot(k1_ref[0, 0, pl.ds(off, tk), :], q1T[:, lo:])
        s2_refs[slot][:, lo:] = _dot(k2_ref[0, 0, pl.ds(off, tk), :], q2T[:, lo:])

    def consume(slot, j, carry, q_lo):
        vT = vT_ref[0, 0, j]
        m1 = _fold_key_block(s1_refs[slot], acc1_ref, vT, carry[0], q_lo)
        m2 = _fold_key_block(s2_refs[slot], acc2_ref, vT, carry[1], q_lo)
        return m1, m2

    def finish(carry):
        l1 = acc1_ref[DIFF_V:DIFF_V + 1, :]
        l2 = acc2_ref[DIFF_V:DIFF_V + 1, :]
        lp = lam_ref[...]
        lam = (jnp.exp(jnp.sum(lp[0:1] * lp[1:2], axis=-1, keepdims=True))
               - jnp.exp(jnp.sum(lp[2:3] * lp[3:4], axis=-1, keepdims=True)) + lam_init)
        oT = acc1_ref[0:DIFF_V, :] / l1 - lam * (acc2_ref[0:DIFF_V, :] / l2)
        o_ref[0] = (_rms(oT.T, sub_ref[...]) * (1.0 - lam_init)).astype(o_ref.dtype)

    acc1_ref[...] = jnp.zeros(acc1_ref.shape, F32)
    acc2_ref[...] = jnp.zeros(acc2_ref.shape, F32)
    m_init = jnp.full((1, tq), NEG_BIG, F32)
    _pipelined_key_blocks(qi, tq // tk, tk, scores, consume, finish, (m_init, m_init))


def _diff_attn(q1T, q2T, k1, k2, vT, lam_params, subln, lam_init):
    b, nh, nq, _, tq = q1T.shape
    nkb, tk = vT.shape[2], vT.shape[4]
    s = nq * tq
    qT_spec = pl.BlockSpec((1, 1, 1, DIFF_DIM, tq), lambda i, h, t: (i, h, t, 0, 0))
    k_spec = pl.BlockSpec((1, 1, s, DIFF_DIM), lambda i, h, t: (i, h, 0, 0))
    return pl.pallas_call(
        functools.partial(_diff_attn_kernel, lam_init),
        grid=(b, nh, nq),
        in_specs=[
            qT_spec, qT_spec, k_spec, k_spec,
            pl.BlockSpec((1, 1, nkb, DIFF_V + SUM_ROWS, tk), lambda i, h, t: (i, h, 0, 0, 0)),
            _const_spec(lam_params.shape),
            _const_spec(subln.shape),
        ],
        out_specs=pl.BlockSpec((1, tq, DIFF_V), lambda i, h, t: (i, t, h)),
        out_shape=jax.ShapeDtypeStruct((b, s, nh * DIFF_V), BF16),
        scratch_shapes=[pltpu.VMEM((tk, tq), F32)] * 4 + [pltpu.VMEM((DIFF_V + SUM_ROWS, tq), F32)] * 2,
        compiler_params=_params(3),
        name="diff_attn",
    )(q1T, q2T, k1, k2, vT, lam_params, subln)


def _out_proj_kernel(o_ref, om_ref, x_ref, w1_ref, w2_ref, g_ref, out_ref):
    mix = _dot(o_ref[0], w1_ref[...]) + _dot(om_ref[0], w2_ref[...])
    out_ref[0] = x_ref[0] + _rms(mix, g_ref[...])


def _out_proj(o, om, x, w1, w2, gain, tm):
    b, s, d = x.shape
    return pl.pallas_call(
        _out_proj_kernel,
        grid=(b, s // tm),
        in_specs=[
            pl.BlockSpec((1, tm, o.shape[-1]), lambda i, t: (i, t, 0)),
            pl.BlockSpec((1, tm, om.shape[-1]), lambda i, t: (i, t, 0)),
            pl.BlockSpec((1, tm, d), lambda i, t: (i, t, 0)),
            _const_spec(w1.shape),
            _const_spec(w2.shape),
            _const_spec(gain.shape),
        ],
        out_specs=pl.BlockSpec((1, tm, d), lambda i, t: (i, t, 0)),
        out_shape=jax.ShapeDtypeStruct((b, s, d), F32),
        compiler_params=_params(2),
        name="out_proj",
    )(o, om, x, w1, w2, gain)


FFN_SUB = 256
FFN_ROWS = 256
NORM_ROWS = 32


def _gelu_tanh(g):
    c = math.sqrt(2.0 / math.pi)
    return 0.5 * g * (1.0 + jnp.tanh(c * (g + 0.044715 * (g * g * g))))


def _causal_conv(u_ref, r, cw, cb):
    base = SUBLANES + r
    u0 = u_ref[base:base + FFN_ROWS, :]
    u1 = u_ref[base - 1:base - 1 + FFN_ROWS, :]
    u2 = u_ref[base - 2:base - 2 + FFN_ROWS, :]
    return cb + cw[0:1] * u2 + cw[1:2] * u1 + cw[2:3] * u0


def _conv_ffn_kernel(x_ref, pre_ref, wg_ref, wv_ref, cwg_ref, cwv_ref, cbg_ref, cbv_ref, wo_ref,
                     post_ref, out_ref, h_ref, acc_ref, tail_ref, u_ref):
    t = pl.program_id(1)
    c = pl.program_id(2)
    nc = pl.num_programs(2)
    tm = x_ref.shape[1]
    fc = wg_ref.shape[1]

    @pl.when(t == 0)
    def _():
        tail_ref[c] = jnp.zeros(tail_ref.shape[1:], F32)

    subs = list(range(0, fc, FFN_SUB))
    groups = [(si, r) for si in range(len(subs)) for r in range(0, tm, FFN_ROWS)]
    per_sub = tm // FFN_ROWS

    def pre_norm():
        for r in range(0, tm, NORM_ROWS):
            rows = slice(r, r + NORM_ROWS)
            h_ref[rows, :] = _rms(x_ref[0, rows, :], pre_ref[...]).astype(BF16)

    def up(si, r):
        lo = subs[si]
        hr = h_ref[r:r + FFN_ROWS, :]
        rows = slice(SUBLANES + r, SUBLANES + r + FFN_ROWS)
        u_ref[si, 0, rows, :] = _dot(hr, wg_ref[:, lo:lo + FFN_SUB])
        u_ref[si, 1, rows, :] = _dot(hr, wv_ref[:, lo:lo + FFN_SUB])

    def chunk(first, last):
        for si, lo in enumerate(subs):
            u_ref[si, 0, 0:SUBLANES, :] = tail_ref[c, 0, :, lo:lo + FFN_SUB]
            u_ref[si, 1, 0:SUBLANES, :] = tail_ref[c, 1, :, lo:lo + FFN_SUB]
        for g in groups:
            up(*g)
        ys = []
        for si, r in groups:
            lo, hi = subs[si], subs[si] + FFN_SUB
            cg = _causal_conv(u_ref.at[si, 0], r, cwg_ref[:, lo:hi], cbg_ref[:, lo:hi])
            cv = _causal_conv(u_ref.at[si, 1], r, cwv_ref[:, lo:hi], cbv_ref[:, lo:hi])
            ys.append((_gelu_tanh(cg) * cv).astype(BF16))
        for si, lo in enumerate(subs):
            wo = wo_ref[lo:lo + FFN_SUB, :]
            if last and si + 1 == len(subs):
                for gi, r in enumerate(range(0, tm, FFN_ROWS)):
                    rows = slice(r, r + FFN_ROWS)
                    total = acc_ref[rows, :] + _dot(ys[si * per_sub + gi], wo)
                    out_ref[0, rows, :] = x_ref[0, rows, :] + _rms(total, post_ref[...])
            else:
                d = _dot(jnp.concatenate(ys[si * per_sub:(si + 1) * per_sub], axis=0), wo)
                if first and si == 0:
                    acc_ref[...] = d
                else:
                    acc_ref[...] += d
        for si, lo in enumerate(subs):
            tail_ref[c, 0, :, lo:lo + FFN_SUB] = u_ref[si, 0, tm:tm + SUBLANES, :]
            tail_ref[c, 1, :, lo:lo + FFN_SUB] = u_ref[si, 1, tm:tm + SUBLANES, :]

    @pl.when(c == 0)
    def _():
        pre_norm()
        chunk(True, False)

    @pl.when(jnp.logical_and(c > 0, c < nc - 1))
    def _():
        chunk(False, False)

    @pl.when(c == nc - 1)
    def _():
        chunk(False, True)


def _conv_ffn(x, pre, w_in, conv_w, conv_b, w_out, post, tm, fc):
    b, s, d = x.shape
    nc = D_FF // fc
    return pl.pallas_call(
        _conv_ffn_kernel,
        grid=(b, s // tm, nc),
        in_specs=[
            pl.BlockSpec((1, tm, d), lambda i, t, c: (i, t, 0)),
            _const_spec(pre.shape),
            pl.BlockSpec((d, fc), lambda i, t, c: (0, c)),
            pl.BlockSpec((d, fc), lambda i, t, c: (0, c + nc)),
            pl.BlockSpec((CONV_WIDTH, fc), lambda i, t, c: (0, c)),
            pl.BlockSpec((CONV_WIDTH, fc), lambda i, t, c: (0, c + nc)),
            pl.BlockSpec((1, fc), lambda i, t, c: (0, c)),
            pl.BlockSpec((1, fc), lambda i, t, c: (0, c + nc)),
            pl.BlockSpec((fc, d), lambda i, t, c: (c, 0)),
            _const_spec(post.shape),
        ],
        out_specs=pl.BlockSpec((1, tm, d), lambda i, t, c: (i, t, 0)),
        out_shape=jax.ShapeDtypeStruct((b, s, d), F32),
        scratch_shapes=[
            pltpu.VMEM((tm, d), BF16),
            pltpu.VMEM((tm, d), F32),
            pltpu.VMEM((nc, 2, SUBLANES, fc), F32),
            pltpu.VMEM((fc // FFN_SUB, 2, SUBLANES + tm, FFN_SUB), F32),
        ],
        compiler_params=_params(3),
        name="conv_ffn",
    )(x, pre, w_in, w_in, conv_w, conv_w, conv_b, conv_b, w_out, post)


def _b_proj_kernel(x_ref, pos_ref, gs_ref, gb_ref, w_k_ref, w_vT_ref, w_qT_ref, w_qm_ref,
                   mkT_ref, mv_ref, q1T_ref, q2T_ref, k1_ref, k2_ref, vT_ref, om_ref):
    xf = x_ref[0]
    n = xf * lax.rsqrt(jnp.mean(xf * xf, axis=-1, keepdims=True) + NORM_EPS)
    hs = (n * gs_ref[...]).astype(BF16)
    hb = (n * gb_ref[...]).astype(BF16)

    half = DIFF_DIM // 2
    cos_r, sin_r = _rope_tables_rows(pos_ref[0], half, DIFF_DIM)

    tab_c = jnp.concatenate([cos_r, cos_r], axis=0).T
    tab_s = jnp.concatenate([-sin_r, sin_r], axis=0).T
    kk = _dot(hs, w_k_ref[...])
    vT = _dot_nt(w_vT_ref[...], hs)
    nk = DIFF_HEADS * DIFF_DIM
    for hd in range(DIFF_HEADS):
        for which, ref in ((0, k1_ref), (1, k2_ref)):
            base = which * nk + hd * DIFF_DIM
            xr = kk[:, base:base + DIFF_DIM]
            ref[0, hd] = (xr * tab_c + pltpu.roll(xr, half, 1) * tab_s).astype(BF16)
        vT_ref[0, hd, 0, 0:DIFF_V, :] = vT[hd * DIFF_V:(hd + 1) * DIFF_V].astype(BF16)
        vT_ref[0, hd, 0, DIFF_V:DIFF_V + SUM_ROWS, :] = jnp.ones((SUM_ROWS, x_ref.shape[1]), BF16)

    scale = DIFF_DIM ** -0.5 * LOG2_E
    cos_q, sin_q = cos_r * scale, sin_r * scale
    qT = _dot_nt(w_qT_ref[...], hb)
    for hd in range(DIFF_HEADS):
        for which, ref in ((0, q1T_ref), (1, q2T_ref)):
            base = hd * 2 * DIFF_DIM + which * DIFF_DIM
            r1, r2 = _rope_rows(qT[base:base + half], qT[base + half:base + DIFF_DIM], cos_q, sin_q)
            ref[0, hd, 0] = jnp.concatenate([r1, r2], axis=0).astype(BF16)

    _mem_attention(_dot(hb, w_qm_ref[...]), mkT_ref, mv_ref, om_ref)


def _b_proj(x, pos_row, gs, gb, w_k, w_vT, w_qT, w_qm, mkT, mv, tm, tq):
    b, s, d = x.shape
    nt = s // tm
    r = tq // tm
    m = mkT.shape[-1]
    qT_shape = jax.ShapeDtypeStruct((b, DIFF_HEADS, s // tq, DIFF_DIM, tq), BF16)
    k_shape = jax.ShapeDtypeStruct((b, DIFF_HEADS, s, DIFF_DIM), BF16)
    qT_spec = pl.BlockSpec((1, DIFF_HEADS, 1, DIFF_DIM, tm), lambda i, t: (i, 0, t // r, 0, t % r))
    k_spec = pl.BlockSpec((1, DIFF_HEADS, tm, DIFF_DIM), lambda i, t: (i, 0, t, 0))
    return pl.pallas_call(
        _b_proj_kernel,
        grid=(b, nt),
        in_specs=[
            pl.BlockSpec((1, tm, d), lambda i, t: (i, t, 0)),
            pl.BlockSpec((1, 1, tm), lambda i, t: (i, 0, t)),
            _const_spec(gs.shape),
            _const_spec(gb.shape),
            _const_spec(w_k.shape),
            _const_spec(w_vT.shape),
            _const_spec(w_qT.shape),
            _const_spec(w_qm.shape),
            pl.BlockSpec((1, 1, MEM_HEADS, MEM_DIM, m), lambda i, t: (1, i, 0, 0, 0)),
            pl.BlockSpec((1, 1, MEM_HEADS, m, MEM_DIM), lambda i, t: (1, i, 0, 0, 0)),
        ],
        out_specs=[
            qT_spec, qT_spec, k_spec, k_spec,
            pl.BlockSpec((1, DIFF_HEADS, 1, DIFF_V + SUM_ROWS, tm), lambda i, t: (i, 0, t, 0, 0)),
            pl.BlockSpec((1, tm, MEM_HEADS * MEM_DIM), lambda i, t: (i, t, 0)),
        ],
        out_shape=[
            qT_shape, qT_shape, k_shape, k_shape,
            jax.ShapeDtypeStruct((b, DIFF_HEADS, nt, DIFF_V + SUM_ROWS, tm), BF16),
            jax.ShapeDtypeStruct((b, s, MEM_HEADS * MEM_DIM), BF16),
        ],
        compiler_params=_params(2),
        name="b_proj",
    )(x, pos_row, gs, gb, w_k, w_vT, w_qT, w_qm, mkT, mv)


def _row(v):
    return v.reshape(1, -1).astype(F32)


def kernel(x, mem, positions, a_attn_norm_pre, a_w_in, a_q_norm, a_w_uq, a_kv_norm, a_w_ukv, a_mem_norm, a_w_mem_kv, a_w_o, a_attn_norm_post, shared_kv_norm, shared_w_kv, b_attn_norm_pre, b_w_in, b_lambda_q1, b_lambda_k1, b_lambda_q2, b_lambda_k2, b_subln, b_mem_norm, b_w_mem_kv, b_w_o, b_attn_norm_post, ffn_norm_pre, ffn_w_in, ffn_conv_w, ffn_conv_b, ffn_w_out, ffn_norm_post):
    b, s, d = x.shape
    assert d == D_MODEL and a_attn_norm_pre.shape[0] == 1 and b_attn_norm_pre.shape[0] == 1
    tm, tq_mla, tq_diff, fc = _tiles(s)
    assert all(s % tq == 0 and tq % tm == 0 for tq in (tq_mla, tq_diff)) and D_FF % fc == 0
    pos_row = positions.reshape(b, 1, s)

    mem_gains = jnp.stack([a_mem_norm[0], b_mem_norm[0]]).reshape(2, 1, d)
    mem_w = jnp.stack([a_w_mem_kv[0], b_w_mem_kv[0]]).astype(BF16)
    mkT, mv = _mem_kv(mem, mem_gains, mem_w)

    w_in = a_w_in[0]
    i2 = Q_LORA + KV_LORA + MLA_ROPE
    w_in_p = jnp.concatenate(
        [w_in[:, :i2], jnp.zeros((d, LANES - MLA_ROPE), F32), w_in[:, i2:]], axis=1).astype(BF16)
    w_uqT = a_w_uq[0].T.astype(BF16)
    w_ukv = a_w_ukv[0].reshape(KV_LORA, MLA_HEADS, MLA_NOPE + MLA_V)
    w_k = w_ukv[:, :, :MLA_NOPE].reshape(KV_LORA, MLA_HEADS * MLA_NOPE).astype(BF16)
    w_vT = w_ukv[:, :, MLA_NOPE:].reshape(KV_LORA, MLA_HEADS * MLA_V).T.astype(BF16)
    qT, k, vT, om = _a_proj(x, pos_row, _row(a_attn_norm_pre[0]), w_in_p, _row(a_q_norm[0]), w_uqT,
                            _row(a_kv_norm[0]), w_k, w_vT, mkT, mv, tm, tq_mla)
    o = _mla_attn(qT, k, vT)
    n_main = MLA_HEADS * MLA_V
    w_o = a_w_o[0].astype(BF16)
    x = _out_proj(o, om, x, w_o[:n_main], w_o[n_main:], _row(a_attn_norm_post[0]), tm)
    x = _conv_ffn(x, _row(ffn_norm_pre[0]), ffn_w_in[0].astype(BF16), ffn_conv_w[0],
                  _row(ffn_conv_b[0]), ffn_w_out[0].astype(BF16), _row(ffn_norm_post[0]), tm, fc)

    layer = N_A_LAYERS
    lam_init = 0.8 - 0.6 * math.exp(-0.3 * layer)
    w_kv = shared_w_kv.reshape(d, DIFF_HEADS, 2 * DIFF_DIM + DIFF_V)
    nk = DIFF_HEADS * DIFF_DIM
    w_k12 = jnp.concatenate(
        [w_kv[:, :, :DIFF_DIM].reshape(d, nk), w_kv[:, :, DIFF_DIM:2 * DIFF_DIM].reshape(d, nk)],
        axis=1).astype(BF16)
    w_svT = w_kv[:, :, 2 * DIFF_DIM:].reshape(d, DIFF_HEADS * DIFF_V).T.astype(BF16)
    nq_cols = DIFF_HEADS * 2 * DIFF_DIM
    w_bq_T = b_w_in[0][:, :nq_cols].T.astype(BF16)
    w_bqm = b_w_in[0][:, nq_cols:].astype(BF16)
    q1T, q2T, k1, k2, vbT, omb = _b_proj(x, pos_row, _row(shared_kv_norm), _row(b_attn_norm_pre[0]),
                                         w_k12, w_svT, w_bq_T, w_bqm, mkT, mv, tm, tq_diff)
    lam_params = jnp.stack([b_lambda_q1[0], b_lambda_k1[0], b_lambda_q2[0], b_lambda_k2[0]]).astype(F32)
    ob = _diff_attn(q1T, q2T, k1, k2, vbT, lam_params, _row(b_subln[0]), lam_init)
    n_main_b = DIFF_HEADS * DIFF_V
    w_ob = b_w_o[0].astype(BF16)
    x = _out_proj(ob, omb, x, w_ob[:n_main_b], w_ob[n_main_b:], _row(b_attn_norm_post[0]), tm)
    x = _conv_ffn(x, _row(ffn_norm_pre[1]), ffn_w_in[1].astype(BF16), ffn_conv_w[1],
                  _row(ffn_conv_b[1]), ffn_w_out[1].astype(BF16), _row(ffn_norm_post[1]), tm, fc)
    return x
```

```python
import functools
import math

import jax
import jax.numpy as jnp
from jax import lax
from jax.experimental import pallas as pl
from jax.experimental.pallas import tpu as pltpu

D_MODEL = 2048
ROPE_THETA = 10000.0
NORM_EPS = 1e-6
MLA_HEADS = 12
MLA_NOPE = 128
MLA_ROPE = 64
MLA_V = 128
MLA_DK = MLA_NOPE + MLA_ROPE
Q_LORA = 512
KV_LORA = 512
DIFF_HEADS = 6
DIFF_DIM = 128
DIFF_V = 2 * DIFF_DIM
MEM_HEADS = 4
MEM_DIM = 128
D_FF = 5632
CONV_WIDTH = 3
N_A_LAYERS = 1

LANES = 128
SUBLANES = 8
SUM_ROWS = 16

V7X_VMEM_BYTES = 64 * 1024 * 1024
VMEM_LIMIT = V7X_VMEM_BYTES - 8 * 1024 * 1024

NEG_BIG = -1e30
LOG2_E = math.log2(math.e)
BF16 = jnp.bfloat16
F32 = jnp.float32


def _tiles(seq):
    tm = min(512, seq)
    tq_mla = min(1024, seq)
    tq_diff = min(1024, seq)
    fc = 512
    return tm, tq_mla, tq_diff, fc


def _params(n_axes):
    return pltpu.CompilerParams(dimension_semantics=("arbitrary",) * n_axes,
                                vmem_limit_bytes=VMEM_LIMIT)


def _const_spec(shape):
    nd = len(shape)
    return pl.BlockSpec(shape, lambda *_: (0,) * nd, pipeline_mode=pl.Buffered(1))


def _rms(xf, gain):
    ms = jnp.mean(xf * xf, axis=-1, keepdims=True)
    return xf * lax.rsqrt(ms + NORM_EPS) * gain


def _dot(a, b):
    return jnp.dot(a, b, preferred_element_type=F32)


def _dot_nt(a, b):
    return lax.dot_general(a, b, (((1,), (1,)), ((), ())), preferred_element_type=F32)


def _rope_tables_rows(pos_row, half, dh):
    tokens = pos_row.shape[-1]
    j = lax.broadcasted_iota(jnp.int32, (half, tokens), 0).astype(F32)
    inv_freq = jnp.exp(j * (-math.log(ROPE_THETA) * 2.0 / dh))
    ang = pos_row.astype(F32) * inv_freq
    return jnp.cos(ang), jnp.sin(ang)


def _rope_rows(a, b, cos_r, sin_r):
    return a * cos_r - b * sin_r, b * cos_r + a * sin_r


def _mem_attention(qm, mkT_ref, mv_ref, out_ref):
    scale = MEM_DIM ** -0.5
    for h in range(MEM_HEADS):
        qh = (qm[:, h * MEM_DIM:(h + 1) * MEM_DIM] * scale).astype(BF16)
        s = _dot(qh, mkT_ref[0, 0, h])
        m = jnp.max(s, axis=-1, keepdims=True)
        p = jnp.exp(s - m)
        l = jnp.sum(p, axis=-1, keepdims=True)
        o = _dot(p.astype(BF16), mv_ref[0, 0, h])
        out_ref[0, :, h * MEM_DIM:(h + 1) * MEM_DIM] = (o / l).astype(out_ref.dtype)


def _mem_kv_kernel(mem_ref, g_ref, w_ref, kT_ref, v_ref):
    n = _rms(mem_ref[0], g_ref[0]).astype(BF16)
    kv = _dot(n, w_ref[0])
    for h in range(MEM_HEADS):
        base = h * 2 * MEM_DIM
        kT_ref[0, 0, h] = kv[:, base:base + MEM_DIM].T.astype(BF16)
        v_ref[0, 0, h] = kv[:, base + MEM_DIM:base + 2 * MEM_DIM].astype(BF16)


def _mem_kv(mem, gains, weights):
    b, m, d = mem.shape
    nl = gains.shape[0]
    return pl.pallas_call(
        _mem_kv_kernel,
        grid=(nl, b),
        in_specs=[
            pl.BlockSpec((1, m, d), lambda l, i: (i, 0, 0)),
            pl.BlockSpec((1, 1, d), lambda l, i: (l, 0, 0)),
            pl.BlockSpec((1, d, MEM_HEADS * 2 * MEM_DIM), lambda l, i: (l, 0, 0)),
        ],
        out_specs=[
            pl.BlockSpec((1, 1, MEM_HEADS, MEM_DIM, m), lambda l, i: (l, i, 0, 0, 0)),
            pl.BlockSpec((1, 1, MEM_HEADS, m, MEM_DIM), lambda l, i: (l, i, 0, 0, 0)),
        ],
        out_shape=[
            jax.ShapeDtypeStruct((nl, b, MEM_HEADS, MEM_DIM, m), BF16),
            jax.ShapeDtypeStruct((nl, b, MEM_HEADS, m, MEM_DIM), BF16),
        ],
        compiler_params=_params(2),
        name="mem_kv",
    )(mem, gains, weights)


def _a_proj_kernel(x_ref, pos_ref, g_ref, w_in_ref, qg_ref, w_uqT_ref, kvg_ref, w_k_ref, w_vT_ref,
                   mkT_ref, mv_ref, qT_ref, k_ref, vT_ref, om_ref):
    tm = x_ref.shape[1]
    h = _rms(x_ref[0], g_ref[...]).astype(BF16)
    proj = _dot(h, w_in_ref[...])
    ql = _rms(proj[:, :Q_LORA], qg_ref[...]).astype(BF16)
    kvl = _rms(proj[:, Q_LORA:Q_LORA + KV_LORA], kvg_ref[...]).astype(BF16)
    pe = proj[:, Q_LORA + KV_LORA:Q_LORA + KV_LORA + LANES]
    qm = proj[:, Q_LORA + KV_LORA + LANES:]

    half = MLA_ROPE // 2
    cos_r, sin_r = _rope_tables_rows(pos_ref[0], half, MLA_ROPE)

    peT = pe.T
    r1, r2 = _rope_rows(peT[0:half], peT[half:2 * half], cos_r, sin_r)
    kpe = jnp.concatenate([r1, r2, jnp.zeros((LANES - MLA_ROPE, tm), F32)], axis=0).T
    kpe = kpe[:, :MLA_ROPE].astype(BF16)

    scale = MLA_DK ** -0.5 * LOG2_E
    cos_q, sin_q = cos_r * scale, sin_r * scale
    qT = _dot_nt(w_uqT_ref[...], ql)
    kn = _dot(kvl, w_k_ref[...])
    vT = _dot_nt(w_vT_ref[...], kvl)
    for hd in range(MLA_HEADS):
        base = hd * MLA_DK
        qT_ref[0, hd, 0, 0:MLA_NOPE, :] = (qT[base:base + MLA_NOPE] * scale).astype(BF16)
        r1, r2 = _rope_rows(qT[base + MLA_NOPE:base + MLA_NOPE + half],
                            qT[base + MLA_NOPE + half:base + MLA_DK], cos_q, sin_q)
        qT_ref[0, hd, 0, MLA_NOPE:MLA_NOPE + half, :] = r1.astype(BF16)
        qT_ref[0, hd, 0, MLA_NOPE + half:MLA_DK, :] = r2.astype(BF16)
        k_ref[0, hd, :, 0:MLA_NOPE] = kn[:, hd * MLA_NOPE:(hd + 1) * MLA_NOPE].astype(BF16)
        k_ref[0, hd, :, MLA_NOPE:MLA_DK] = kpe
        vT_ref[0, hd, 0, 0:MLA_V, :] = vT[hd * MLA_V:(hd + 1) * MLA_V].astype(BF16)
        vT_ref[0, hd, 0, MLA_V:MLA_V + SUM_ROWS, :] = jnp.ones((SUM_ROWS, tm), BF16)

    _mem_attention(qm, mkT_ref, mv_ref, om_ref)


def _a_proj(x, pos_row, gain, w_in, qg, w_uqT, kvg, w_k, w_vT, mkT, mv, tm, tq):
    b, s, d = x.shape
    nt = s // tm
    r = tq // tm
    m = mkT.shape[-1]
    return pl.pallas_call(
        _a_proj_kernel,
        grid=(b, nt),
        in_specs=[
            pl.BlockSpec((1, tm, d), lambda i, t: (i, t, 0)),
            pl.BlockSpec((1, 1, tm), lambda i, t: (i, 0, t)),
            _const_spec(gain.shape),
            _const_spec(w_in.shape),
            _const_spec(qg.shape),
            _const_spec(w_uqT.shape),
            _const_spec(kvg.shape),
            _const_spec(w_k.shape),
            _const_spec(w_vT.shape),
            pl.BlockSpec((1, 1, MEM_HEADS, MEM_DIM, m), lambda i, t: (0, i, 0, 0, 0)),
            pl.BlockSpec((1, 1, MEM_HEADS, m, MEM_DIM), lambda i, t: (0, i, 0, 0, 0)),
        ],
        out_specs=[
            pl.BlockSpec((1, MLA_HEADS, 1, MLA_DK, tm), lambda i, t: (i, 0, t // r, 0, t % r)),
            pl.BlockSpec((1, MLA_HEADS, tm, MLA_DK), lambda i, t: (i, 0, t, 0)),
            pl.BlockSpec((1, MLA_HEADS, 1, MLA_V + SUM_ROWS, tm), lambda i, t: (i, 0, t, 0, 0)),
            pl.BlockSpec((1, tm, MEM_HEADS * MEM_DIM), lambda i, t: (i, t, 0)),
        ],
        out_shape=[
            jax.ShapeDtypeStruct((b, MLA_HEADS, s // tq, MLA_DK, tq), BF16),
            jax.ShapeDtypeStruct((b, MLA_HEADS, s, MLA_DK), BF16),
            jax.ShapeDtypeStruct((b, MLA_HEADS, nt, MLA_V + SUM_ROWS, tm), BF16),
            jax.ShapeDtypeStruct((b, s, MEM_HEADS * MEM_DIM), BF16),
        ],
        compiler_params=_params(2),
        name="a_proj",
    )(x, pos_row, gain, w_in, qg, w_uqT, kvg, w_k, w_vT, mkT, mv)


def _online_softmax_pv(s_ref, acc_ref, vT, cols, m_prev, masked):
    def read():
        s = s_ref[:, cols]
        if masked:
            key = lax.broadcasted_iota(jnp.int32, s.shape, 0)
            qry = lax.broadcasted_iota(jnp.int32, s.shape, 1)
            s = jnp.where(key <= qry, s, NEG_BIG)
        return s

    m_new = jnp.maximum(m_prev, jnp.max(read(), axis=0, keepdims=True))
    acc_ref[:, cols] = jnp.exp2(m_prev - m_new) * acc_ref[:, cols]
    p = jnp.exp2(read() - m_new).astype(BF16)
    acc_ref[:, cols] += _dot(vT, p)
    return m_new


def _fold_key_block(s_ref, acc_ref, vT, m_prev, q_lo):
    if q_lo is None:
        qc = min(QUERY_CHUNK, s_ref.shape[1])
        parts = []
        for c0 in range(0, s_ref.shape[1], qc):
            cols = slice(c0, c0 + qc)
            parts.append(_online_softmax_pv(s_ref, acc_ref, vT, cols, m_prev[:, cols], False))
        return jnp.concatenate(parts, axis=1)
    m_part = _online_softmax_pv(s_ref, acc_ref, vT, slice(q_lo, None), m_prev[:, q_lo:], True)
    return m_part if q_lo == 0 else jnp.concatenate([m_prev[:, :q_lo], m_part], axis=1)


QUERY_CHUNK = 512
MLA_KEY_UNROLL = 8
DIFF_KEY_UNROLL = 4


def _pipelined_key_blocks(qi, nd, tk, unroll, scores, consume, finish, init):
    n_full = qi * nd
    scores(0, 0, None)

    def group(g, carry):
        j = unroll * g
        for u in range(unroll):
            scores(j + u + 1, (u + 1) % 2, None)
            carry = consume(u % 2, j + u, carry, None)
        return carry

    carry = lax.fori_loop(0, n_full // unroll, group, init)
    base = (n_full // unroll) * unroll

    for rem in range(0, unroll, math.gcd(nd, unroll)):
        @pl.when(n_full % unroll == rem)
        def _(rem=rem):
            blocks = [(base + u, None) for u in range(rem)] + [(n_full + d, d * tk) for d in range(nd)]
            c = carry
            for idx, (j, q_lo) in enumerate(blocks):
                if idx + 1 < len(blocks):
                    scores(blocks[idx + 1][0], (idx + 1) % 2, blocks[idx + 1][1])
                c = consume(idx % 2, j, c, q_lo)
            finish(c)


def _mla_attn_kernel(qT_ref, k_ref, vT_ref, o_ref, sa_ref, sb_ref, acc_ref):
    tq = qT_ref.shape[4]
    tk = vT_ref.shape[4]
    qi = pl.program_id(2)
    qT = qT_ref[0, 0, 0]
    s_refs = (sa_ref, sb_ref)

    def scores(j, slot, q_lo):
        off = pl.multiple_of(j * tk, tk)
        lo = q_lo or 0
        s_refs[slot][:, lo:] = _dot(k_ref[0, 0, pl.ds(off, tk), :], qT[:, lo:])

    def consume(slot, j, carry, q_lo):
        return _fold_key_block(s_refs[slot], acc_ref, vT_ref[0, 0, j], carry, q_lo)

    def finish(carry):
        l = acc_ref[MLA_V:MLA_V + 1, :]
        o_ref[0] = (acc_ref[0:MLA_V, :] / l).T.astype(o_ref.dtype)

    acc_ref[...] = jnp.zeros(acc_ref.shape, F32)
    init = jnp.full((1, tq), NEG_BIG, F32)
    _pipelined_key_blocks(qi, tq // tk, tk, MLA_KEY_UNROLL, scores, consume, finish, init)


def _mla_attn(qT, k, vT):
    b, nh, nq, _, tq = qT.shape
    nkb, tk = vT.shape[2], vT.shape[4]
    s = nq * tq
    return pl.pallas_call(
        _mla_attn_kernel,
        grid=(b, nh, nq),
        in_specs=[
            pl.BlockSpec((1, 1, 1, MLA_DK, tq), lambda i, h, t: (i, h, t, 0, 0)),
            pl.BlockSpec((1, 1, s, MLA_DK), lambda i, h, t: (i, h, 0, 0)),
            pl.BlockSpec((1, 1, nkb, MLA_V + SUM_ROWS, tk), lambda i, h, t: (i, h, 0, 0, 0)),
        ],
        out_specs=pl.BlockSpec((1, tq, MLA_V), lambda i, h, t: (i, t, h)),
        out_shape=jax.ShapeDtypeStruct((b, s, nh * MLA_V), BF16),
        scratch_shapes=[pltpu.VMEM((tk, tq), F32), pltpu.VMEM((tk, tq), F32),
                        pltpu.VMEM((MLA_V + SUM_ROWS, tq), F32)],
        compiler_params=_params(3),
        name="mla_attn",
    )(qT, k, vT)


def _diff_attn_kernel(lam_init, q1T_ref, q2T_ref, k1_ref, k2_ref, vT_ref, lam_ref, sub_ref, o_ref,
                      s1a_ref, s1b_ref, s2a_ref, s2b_ref, acc1_ref, acc2_ref):
    tq = q1T_ref.shape[4]
    tk = vT_ref.shape[4]
    qi = pl.program_id(2)
    q1T = q1T_ref[0, 0, 0]
    q2T = q2T_ref[0, 0, 0]
    s1_refs = (s1a_ref, s1b_ref)
    s2_refs = (s2a_ref, s2b_ref)

    def scores(j, slot, q_lo):
        off = pl.multiple_of(j * tk, tk)
        lo = q_lo or 0
        s1_refs[slot][:, lo:] = _dot(k1_ref[0, 0, pl.ds(off, tk), :], q1T[:, lo:])
        s2_refs[slot][:, lo:] = _dot(k2_ref[0, 0, pl.ds(off, tk), :], q2T[:, lo:])

    def consume(slot, j, carry, q_lo):
        vT = vT_ref[0, 0, j]
        m1 = _fold_key_block(s1_refs[slot], acc1_ref, vT, carry[0], q_lo)
        m2 = _fold_key_block(s2_refs[slot], acc2_ref, vT, carry[1], q_lo)
        return m1, m2

    def finish(carry):
        l1 = acc1_ref[DIFF_V:DIFF_V + 1, :]
        l2 = acc2_ref[DIFF_V:DIFF_V + 1, :]
        lp = lam_ref[...]
        lam = (jnp.exp(jnp.sum(lp[0:1] * lp[1:2], axis=-1, keepdims=True))
               - jnp.exp(jnp.sum(lp[2:3] * lp[3:4], axis=-1, keepdims=True)) + lam_init)
        oT = acc1_ref[0:DIFF_V, :] / l1 - lam * (acc2_ref[0:DIFF_V, :] / l2)
        o_ref[0] = (_rms(oT.T, sub_ref[...]) * (1.0 - lam_init)).astype(o_ref.dtype)

    acc1_ref[...] = jnp.zeros(acc1_ref.shape, F32)
    acc2_ref[...] = jnp.zeros(acc2_ref.shape, F32)
    m_init = jnp.full((1, tq), NEG_BIG, F32)
    _pipelined_key_blocks(qi, tq // tk, tk, DIFF_KEY_UNROLL, scores, consume, finish, (m_init, m_init))


def _diff_attn(q1T, q2T, k1, k2, vT, lam_params, subln, lam_init):
    b, nh, nq, _, tq = q1T.shape
    nkb, tk = vT.shape[2], vT.shape[4]
    s = nq * tq
    qT_spec = pl.BlockSpec((1, 1, 1, DIFF_DIM, tq), lambda i, h, t: (i, h, t, 0, 0))
    k_spec = pl.BlockSpec((1, 1, s, DIFF_DIM), lambda i, h, t: (i, h, 0, 0))
    return pl.pallas_call(
        functools.partial(_diff_attn_kernel, lam_init),
        grid=(b, nh, nq),
        in_specs=[
            qT_spec, qT_spec, k_spec, k_spec,
            pl.BlockSpec((1, 1, nkb, DIFF_V + SUM_ROWS, tk), lambda i, h, t: (i, h, 0, 0, 0)),
            _const_spec(lam_params.shape),
            _const_spec(subln.shape),
        ],
        out_specs=pl.BlockSpec((1, tq, DIFF_V), lambda i, h, t: (i, t, h)),
        out_shape=jax.ShapeDtypeStruct((b, s, nh * DIFF_V), BF16),
        scratch_shapes=[pltpu.VMEM((tk, tq), F32)] * 4 + [pltpu.VMEM((DIFF_V + SUM_ROWS, tq), F32)] * 2,
        compiler_params=_params(3),
        name="diff_attn",
    )(q1T, q2T, k1, k2, vT, lam_params, subln)


def _out_proj_kernel(o_ref, om_ref, x_ref, w1_ref, w2_ref, g_ref, out_ref):
    mix = _dot(o_ref[0], w1_ref[...]) + _dot(om_ref[0], w2_ref[...])
    out_ref[0] = x_ref[0] + _rms(mix, g_ref[...])


def _out_proj(o, om, x, w1, w2, gain, tm):
    b, s, d = x.shape
    return pl.pallas_call(
        _out_proj_kernel,
        grid=(b, s // tm),
        in_specs=[
            pl.BlockSpec((1, tm, o.shape[-1]), lambda i, t: (i, t, 0)),
            pl.BlockSpec((1, tm, om.shape[-1]), lambda i, t: (i, t, 0)),
            pl.BlockSpec((1, tm, d), lambda i, t: (i, t, 0)),
            _const_spec(w1.shape),
            _const_spec(w2.shape),
            _const_spec(gain.shape),
        ],
        out_specs=pl.BlockSpec((1, tm, d), lambda i, t: (i, t, 0)),
        out_shape=jax.ShapeDtypeStruct((b, s, d), F32),
        compiler_params=_params(2),
        name="out_proj",
    )(o, om, x, w1, w2, gain)


FFN_SUB = 256
FFN_ROWS = 256
NORM_ROWS = 32


def _gelu_tanh(g):
    c = math.sqrt(2.0 / math.pi)
    return 0.5 * g * (1.0 + jnp.tanh(c * (g + 0.044715 * (g * g * g))))


def _causal_conv(u_ref, r, cw, cb):
    base = SUBLANES + r
    u0 = u_ref[base:base + FFN_ROWS, :]
    u1 = u_ref[base - 1:base - 1 + FFN_ROWS, :]
    u2 = u_ref[base - 2:base - 2 + FFN_ROWS, :]
    return cb + cw[0:1] * u2 + cw[1:2] * u1 + cw[2:3] * u0


def _conv_ffn_kernel(x_ref, pre_ref, wg_ref, wv_ref, cwg_ref, cwv_ref, cbg_ref, cbv_ref, wo_ref,
                     post_ref, out_ref, h_ref, acc_ref, tail_ref, u_ref):
    t = pl.program_id(1)
    c = pl.program_id(2)
    nc = pl.num_programs(2)
    tm = x_ref.shape[1]
    fc = wg_ref.shape[1]

    @pl.when(c == 0)
    def _():
        h_ref[...] = _rms(x_ref[0], pre_ref[...]).astype(BF16)
        acc_ref[...] = jnp.zeros(acc_ref.shape, F32)

    @pl.when(t == 0)
    def _():
        tail_ref[c] = jnp.zeros(tail_ref.shape[1:], F32)

    subs = list(range(0, fc, FFN_SUB))
    for si, lo in enumerate(subs):
        u_ref[si, 0, 0:SUBLANES, :] = tail_ref[c, 0, :, lo:lo + FFN_SUB]
        u_ref[si, 1, 0:SUBLANES, :] = tail_ref[c, 1, :, lo:lo + FFN_SUB]

    groups = [(si, r) for si in range(len(subs)) for r in range(0, tm, FFN_ROWS)]

    def up(si, r):
        lo = subs[si]
        hr = h_ref[r:r + FFN_ROWS, :]
        rows = slice(SUBLANES + r, SUBLANES + r + FFN_ROWS)
        u_ref[si, 0, rows, :] = _dot(hr, wg_ref[:, lo:lo + FFN_SUB])
        u_ref[si, 1, rows, :] = _dot(hr, wv_ref[:, lo:lo + FFN_SUB])

    for g in groups:
        up(*g)
    ys = []
    for si, r in groups:
        lo, hi = subs[si], subs[si] + FFN_SUB
        cg = _causal_conv(u_ref.at[si, 0], r, cwg_ref[:, lo:hi], cbg_ref[:, lo:hi])
        cv = _causal_conv(u_ref.at[si, 1], r, cwv_ref[:, lo:hi], cbv_ref[:, lo:hi])
        ys.append((_gelu_tanh(cg) * cv).astype(BF16))
    for (si, r), y in zip(groups, ys):
        lo, hi = subs[si], subs[si] + FFN_SUB
        acc_ref[r:r + FFN_ROWS, :] += _dot(y, wo_ref[lo:hi, :])

    for si, lo in enumerate(subs):
        tail_ref[c, 0, :, lo:lo + FFN_SUB] = u_ref[si, 0, tm:tm + SUBLANES, :]
        tail_ref[c, 1, :, lo:lo + FFN_SUB] = u_ref[si, 1, tm:tm + SUBLANES, :]

    @pl.when(c == nc - 1)
    def _():
        for r in range(0, tm, NORM_ROWS):
            rows = slice(r, r + NORM_ROWS)
            out_ref[0, rows, :] = x_ref[0, rows, :] + _rms(acc_ref[rows, :], post_ref[...])


def _conv_ffn(x, pre, w_in, conv_w, conv_b, w_out, post, tm, fc):
    b, s, d = x.shape
    nc = D_FF // fc
    return pl.pallas_call(
        _conv_ffn_kernel,
        grid=(b, s // tm, nc),
        in_specs=[
            pl.BlockSpec((1, tm, d), lambda i, t, c: (i, t, 0)),
            _const_spec(pre.shape),
            pl.BlockSpec((d, fc), lambda i, t, c: (0, c)),
            pl.BlockSpec((d, fc), lambda i, t, c: (0, c + nc)),
            pl.BlockSpec((CONV_WIDTH, fc), lambda i, t, c: (0, c)),
            pl.BlockSpec((CONV_WIDTH, fc), lambda i, t, c: (0, c + nc)),
            pl.BlockSpec((1, fc), lambda i, t, c: (0, c)),
            pl.BlockSpec((1, fc), lambda i, t, c: (0, c + nc)),
            pl.BlockSpec((fc, d), lambda i, t, c: (c, 0)),
            _const_spec(post.shape),
        ],
        out_specs=pl.BlockSpec((1, tm, d), lambda i, t, c: (i, t, 0)),
        out_shape=jax.ShapeDtypeStruct((b, s, d), F32),
        scratch_shapes=[
            pltpu.VMEM((tm, d), BF16),
            pltpu.VMEM((tm, d), F32),
            pltpu.VMEM((nc, 2, SUBLANES, fc), F32),
            pltpu.VMEM((fc // FFN_SUB, 2, SUBLANES + tm, FFN_SUB), F32),
        ],
        compiler_params=_params(3),
        name="conv_ffn",
    )(x, pre, w_in, w_in, conv_w, conv_w, conv_b, conv_b, w_out, post)


def _b_proj_kernel(x_ref, pos_ref, gs_ref, gb_ref, w_k_ref, w_vT_ref, w_qT_ref, w_qm_ref,
                   mkT_ref, mv_ref, q1T_ref, q2T_ref, k1_ref, k2_ref, vT_ref, om_ref):
    xf = x_ref[0]
    n = xf * lax.rsqrt(jnp.mean(xf * xf, axis=-1, keepdims=True) + NORM_EPS)
    hs = (n * gs_ref[...]).astype(BF16)
    hb = (n * gb_ref[...]).astype(BF16)

    half = DIFF_DIM // 2
    cos_r, sin_r = _rope_tables_rows(pos_ref[0], half, DIFF_DIM)

    tab_c = jnp.concatenate([cos_r, cos_r], axis=0).T
    tab_s = jnp.concatenate([-sin_r, sin_r], axis=0).T
    kk = _dot(hs, w_k_ref[...])
    vT = _dot_nt(w_vT_ref[...], hs)
    nk = DIFF_HEADS * DIFF_DIM
    for hd in range(DIFF_HEADS):
        for which, ref in ((0, k1_ref), (1, k2_ref)):
            base = which * nk + hd * DIFF_DIM
            xr = kk[:, base:base + DIFF_DIM]
            ref[0, hd] = (xr * tab_c + pltpu.roll(xr, half, 1) * tab_s).astype(BF16)
        vT_ref[0, hd, 0, 0:DIFF_V, :] = vT[hd * DIFF_V:(hd + 1) * DIFF_V].astype(BF16)
        vT_ref[0, hd, 0, DIFF_V:DIFF_V + SUM_ROWS, :] = jnp.ones((SUM_ROWS, x_ref.shape[1]), BF16)

    scale = DIFF_DIM ** -0.5 * LOG2_E
    cos_q, sin_q = cos_r * scale, sin_r * scale
    qT = _dot_nt(w_qT_ref[...], hb)
    for hd in range(DIFF_HEADS):
        for which, ref in ((0, q1T_ref), (1, q2T_ref)):
            base = hd * 2 * DIFF_DIM + which * DIFF_DIM
            r1, r2 = _rope_rows(qT[base:base + half], qT[base + half:base + DIFF_DIM], cos_q, sin_q)
            ref[0, hd, 0] = jnp.concatenate([r1, r2], axis=0).astype(BF16)

    _mem_attention(_dot(hb, w_qm_ref[...]), mkT_ref, mv_ref, om_ref)


def _b_proj(x, pos_row, gs, gb, w_k, w_vT, w_qT, w_qm, mkT, mv, tm, tq):
    b, s, d = x.shape
    nt = s // tm
    r = tq // tm
    m = mkT.shape[-1]
    qT_shape = jax.ShapeDtypeStruct((b, DIFF_HEADS, s // tq, DIFF_DIM, tq), BF16)
    k_shape = jax.ShapeDtypeStruct((b, DIFF_HEADS, s, DIFF_DIM), BF16)
    qT_spec = pl.BlockSpec((1, DIFF_HEADS, 1, DIFF_DIM, tm), lambda i, t: (i, 0, t // r, 0, t % r))
    k_spec = pl.BlockSpec((1, DIFF_HEADS, tm, DIFF_DIM), lambda i, t: (i, 0, t, 0))
    return pl.pallas_call(
        _b_proj_kernel,
        grid=(b, nt),
        in_specs=[
            pl.BlockSpec((1, tm, d), lambda i, t: (i, t, 0)),
            pl.BlockSpec((1, 1, tm), lambda i, t: (i, 0, t)),
            _const_spec(gs.shape),
            _const_spec(gb.shape),
            _const_spec(w_k.shape),
            _const_spec(w_vT.shape),
            _const_spec(w_qT.shape),
            _const_spec(w_qm.shape),
            pl.BlockSpec((1, 1, MEM_HEADS, MEM_DIM, m), lambda i, t: (1, i, 0, 0, 0)),
            pl.BlockSpec((1, 1, MEM_HEADS, m, MEM_DIM), lambda i, t: (1, i, 0, 0, 0)),
        ],
        out_specs=[
            qT_spec, qT_spec, k_spec, k_spec,
            pl.BlockSpec((1, DIFF_HEADS, 1, DIFF_V + SUM_ROWS, tm), lambda i, t: (i, 0, t, 0, 0)),
            pl.BlockSpec((1, tm, MEM_HEADS * MEM_DIM), lambda i, t: (i, t, 0)),
        ],
        out_shape=[
            qT_shape, qT_shape, k_shape, k_shape,
            jax.ShapeDtypeStruct((b, DIFF_HEADS, nt, DIFF_V + SUM_ROWS, tm), BF16),
            jax.ShapeDtypeStruct((b, s, MEM_HEADS * MEM_DIM), BF16),
        ],
        compiler_params=_params(2),
        name="b_proj",
    )(x, pos_row, gs, gb, w_k, w_vT, w_qT, w_qm, mkT, mv)


def _row(v):
    return v.reshape(1, -1).astype(F32)


def kernel(x, mem, positions, a_attn_norm_pre, a_w_in, a_q_norm, a_w_uq, a_kv_norm, a_w_ukv, a_mem_norm, a_w_mem_kv, a_w_o, a_attn_norm_post, shared_kv_norm, shared_w_kv, b_attn_norm_pre, b_w_in, b_lambda_q1, b_lambda_k1, b_lambda_q2, b_lambda_k2, b_subln, b_mem_norm, b_w_mem_kv, b_w_o, b_attn_norm_post, ffn_norm_pre, ffn_w_in, ffn_conv_w, ffn_conv_b, ffn_w_out, ffn_norm_post):
    b, s, d = x.shape
    assert d == D_MODEL and a_attn_norm_pre.shape[0] == 1 and b_attn_norm_pre.shape[0] == 1
    tm, tq_mla, tq_diff, fc = _tiles(s)
    assert all(s % tq == 0 and tq % tm == 0 for tq in (tq_mla, tq_diff)) and D_FF % fc == 0
    pos_row = positions.reshape(b, 1, s)

    mem_gains = jnp.stack([a_mem_norm[0], b_mem_norm[0]]).reshape(2, 1, d)
    mem_w = jnp.stack([a_w_mem_kv[0], b_w_mem_kv[0]]).astype(BF16)
    mkT, mv = _mem_kv(mem, mem_gains, mem_w)

    w_in = a_w_in[0]
    i2 = Q_LORA + KV_LORA + MLA_ROPE
    w_in_p = jnp.concatenate(
        [w_in[:, :i2], jnp.zeros((d, LANES - MLA_ROPE), F32), w_in[:, i2:]], axis=1).astype(BF16)
    w_uqT = a_w_uq[0].T.astype(BF16)
    w_ukv = a_w_ukv[0].reshape(KV_LORA, MLA_HEADS, MLA_NOPE + MLA_V)
    w_k = w_ukv[:, :, :MLA_NOPE].reshape(KV_LORA, MLA_HEADS * MLA_NOPE).astype(BF16)
    w_vT = w_ukv[:, :, MLA_NOPE:].reshape(KV_LORA, MLA_HEADS * MLA_V).T.astype(BF16)
    qT, k, vT, om = _a_proj(x, pos_row, _row(a_attn_norm_pre[0]), w_in_p, _row(a_q_norm[0]), w_uqT,
                            _row(a_kv_norm[0]), w_k, w_vT, mkT, mv, tm, tq_mla)
    o = _mla_attn(qT, k, vT)
    n_main = MLA_HEADS * MLA_V
    w_o = a_w_o[0].astype(BF16)
    x = _out_proj(o, om, x, w_o[:n_main], w_o[n_main:], _row(a_attn_norm_post[0]), tm)
    x = _conv_ffn(x, _row(ffn_norm_pre[0]), ffn_w_in[0].astype(BF16), ffn_conv_w[0],
                  _row(ffn_conv_b[0]), ffn_w_out[0].astype(BF16), _row(ffn_norm_post[0]), tm, fc)

    layer = N_A_LAYERS
    lam_init = 0.8 - 0.6 * math.exp(-0.3 * layer)
    w_kv = shared_w_kv.reshape(d, DIFF_HEADS, 2 * DIFF_DIM + DIFF_V)
    nk = DIFF_HEADS * DIFF_DIM
    w_k12 = jnp.concatenate(
        [w_kv[:, :, :DIFF_DIM].reshape(d, nk), w_kv[:, :, DIFF_DIM:2 * DIFF_DIM].reshape(d, nk)],
        axis=1).astype(BF16)
    w_svT = w_kv[:, :, 2 * DIFF_DIM:].reshape(d, DIFF_HEADS * DIFF_V).T.astype(BF16)
    nq_cols = DIFF_HEADS * 2 * DIFF_DIM
    w_bq_T = b_w_in[0][:, :nq_cols].T.astype(BF16)
    w_bqm = b_w_in[0][:, nq_cols:].astype(BF16)
    q1T, q2T, k1, k2, vbT, omb = _b_proj(x, pos_row, _row(shared_kv_norm), _row(b_attn_norm_pre[0]),
                                         w_k12, w_svT, w_bq_T, w_bqm, mkT, mv, tm, tq_diff)
    lam_params = jnp.stack([b_lambda_q1[0], b_lambda_k1[0], b_lambda_q2[0], b_lambda_k2[0]]).astype(F32)
    ob = _diff_attn(q1T, q2T, k1, k2, vbT, lam_params, _row(b_subln[0]), lam_init)
    n_main_b = DIFF_HEADS * DIFF_V
    w_ob = b_w_o[0].astype(BF16)
    x = _out_proj(ob, omb, x, w_ob[:n_main_b], w_ob[n_main_b:], _row(b_attn_norm_post[0]), tm)
    x = _conv_ffn(x, _row(ffn_norm_pre[1]), ffn_w_in[1].astype(BF16), ffn_conv_w[1],
                  _row(ffn_conv_b[1]), ffn_w_out[1].astype(BF16), _row(ffn_norm_post[1]), tm, fc)
    return x
```

```python
import functools
import math

import jax
import jax.numpy as jnp
from jax import lax
from jax.experimental import pallas as pl
from jax.experimental.pallas import tpu as pltpu

D_MODEL = 2048
ROPE_THETA = 10000.0
NORM_EPS = 1e-6
MLA_HEADS = 12
MLA_NOPE = 128
MLA_ROPE = 64
MLA_V = 128
MLA_DK = MLA_NOPE + MLA_ROPE
Q_LORA = 512
KV_LORA = 512
DIFF_HEADS = 6
DIFF_DIM = 128
DIFF_V = 2 * DIFF_DIM
MEM_HEADS = 4
MEM_DIM = 128
D_FF = 5632
CONV_WIDTH = 3
N_A_LAYERS = 1

LANES = 128
SUBLANES = 8
SUM_ROWS = 16

V7X_VMEM_BYTES = 64 * 1024 * 1024
VMEM_LIMIT = V7X_VMEM_BYTES - 8 * 1024 * 1024

NEG_BIG = -1e30
LOG2_E = math.log2(math.e)
BF16 = jnp.bfloat16
F32 = jnp.float32


def _tiles(seq):
    tm = min(512, seq)
    tq_mla = min(1024, seq)
    tq_diff = min(1024, seq)
    fc = 512
    return tm, tq_mla, tq_diff, fc


def _params(n_axes):
    return pltpu.CompilerParams(dimension_semantics=("arbitrary",) * n_axes,
                                vmem_limit_bytes=VMEM_LIMIT)


def _const_spec(shape):
    nd = len(shape)
    return pl.BlockSpec(shape, lambda *_: (0,) * nd, pipeline_mode=pl.Buffered(1))


def _rms(xf, gain):
    ms = jnp.mean(xf * xf, axis=-1, keepdims=True)
    return xf * lax.rsqrt(ms + NORM_EPS) * gain


def _dot(a, b):
    return jnp.dot(a, b, preferred_element_type=F32)


def _dot_nt(a, b):
    return lax.dot_general(a, b, (((1,), (1,)), ((), ())), preferred_element_type=F32)


def _rope_tables_rows(pos_row, half, dh):
    tokens = pos_row.shape[-1]
    j = lax.broadcasted_iota(jnp.int32, (half, tokens), 0).astype(F32)
    inv_freq = jnp.exp(j * (-math.log(ROPE_THETA) * 2.0 / dh))
    ang = pos_row.astype(F32) * inv_freq
    return jnp.cos(ang), jnp.sin(ang)


def _rope_rows(a, b, cos_r, sin_r):
    return a * cos_r - b * sin_r, b * cos_r + a * sin_r


def _mem_attention(qm, mkT_ref, mv_ref, out_ref):
    scale = MEM_DIM ** -0.5
    for h in range(MEM_HEADS):
        qh = (qm[:, h * MEM_DIM:(h + 1) * MEM_DIM] * scale).astype(BF16)
        s = _dot(qh, mkT_ref[0, 0, h])
        m = jnp.max(s, axis=-1, keepdims=True)
        p = jnp.exp(s - m)
        l = jnp.sum(p, axis=-1, keepdims=True)
        o = _dot(p.astype(BF16), mv_ref[0, 0, h])
        out_ref[0, :, h * MEM_DIM:(h + 1) * MEM_DIM] = (o / l).astype(out_ref.dtype)


def _mem_kv_kernel(mem_ref, g_ref, w_ref, kT_ref, v_ref):
    n = _rms(mem_ref[0], g_ref[0]).astype(BF16)
    kv = _dot(n, w_ref[0])
    for h in range(MEM_HEADS):
        base = h * 2 * MEM_DIM
        kT_ref[0, 0, h] = kv[:, base:base + MEM_DIM].T.astype(BF16)
        v_ref[0, 0, h] = kv[:, base + MEM_DIM:base + 2 * MEM_DIM].astype(BF16)


def _mem_kv(mem, gains, weights):
    b, m, d = mem.shape
    nl = gains.shape[0]
    return pl.pallas_call(
        _mem_kv_kernel,
        grid=(nl, b),
        in_specs=[
            pl.BlockSpec((1, m, d), lambda l, i: (i, 0, 0)),
            pl.BlockSpec((1, 1, d), lambda l, i: (l, 0, 0)),
            pl.BlockSpec((1, d, MEM_HEADS * 2 * MEM_DIM), lambda l, i: (l, 0, 0)),
        ],
        out_specs=[
            pl.BlockSpec((1, 1, MEM_HEADS, MEM_DIM, m), lambda l, i: (l, i, 0, 0, 0)),
            pl.BlockSpec((1, 1, MEM_HEADS, m, MEM_DIM), lambda l, i: (l, i, 0, 0, 0)),
        ],
        out_shape=[
            jax.ShapeDtypeStruct((nl, b, MEM_HEADS, MEM_DIM, m), BF16),
            jax.ShapeDtypeStruct((nl, b, MEM_HEADS, m, MEM_DIM), BF16),
        ],
        compiler_params=_params(2),
        name="mem_kv",
    )(mem, gains, weights)


def _a_proj_kernel(x_ref, pos_ref, g_ref, w_in_ref, qg_ref, w_uqT_ref, kvg_ref, w_k_ref, w_vT_ref,
                   mkT_ref, mv_ref, qT_ref, k_ref, vT_ref, om_ref):
    tm = x_ref.shape[1]
    h = _rms(x_ref[0], g_ref[...]).astype(BF16)
    proj = _dot(h, w_in_ref[...])
    ql = _rms(proj[:, :Q_LORA], qg_ref[...]).astype(BF16)
    kvl = _rms(proj[:, Q_LORA:Q_LORA + KV_LORA], kvg_ref[...]).astype(BF16)
    pe = proj[:, Q_LORA + KV_LORA:Q_LORA + KV_LORA + LANES]
    qm = proj[:, Q_LORA + KV_LORA + LANES:]

    half = MLA_ROPE // 2
    cos_r, sin_r = _rope_tables_rows(pos_ref[0], half, MLA_ROPE)

    peT = pe.T
    r1, r2 = _rope_rows(peT[0:half], peT[half:2 * half], cos_r, sin_r)
    kpe = jnp.concatenate([r1, r2, jnp.zeros((LANES - MLA_ROPE, tm), F32)], axis=0).T
    kpe = kpe[:, :MLA_ROPE].astype(BF16)

    scale = MLA_DK ** -0.5 * LOG2_E
    cos_q, sin_q = cos_r * scale, sin_r * scale
    qT = _dot_nt(w_uqT_ref[...], ql)
    kn = _dot(kvl, w_k_ref[...])
    vT = _dot_nt(w_vT_ref[...], kvl)
    for hd in range(MLA_HEADS):
        base = hd * MLA_DK
        qT_ref[0, hd, 0, 0:MLA_NOPE, :] = (qT[base:base + MLA_NOPE] * scale).astype(BF16)
        r1, r2 = _rope_rows(qT[base + MLA_NOPE:base + MLA_NOPE + half],
                            qT[base + MLA_NOPE + half:base + MLA_DK], cos_q, sin_q)
        qT_ref[0, hd, 0, MLA_NOPE:MLA_NOPE + half, :] = r1.astype(BF16)
        qT_ref[0, hd, 0, MLA_NOPE + half:MLA_DK, :] = r2.astype(BF16)
        k_ref[0, hd, :, 0:MLA_NOPE] = kn[:, hd * MLA_NOPE:(hd + 1) * MLA_NOPE].astype(BF16)
        k_ref[0, hd, :, MLA_NOPE:MLA_DK] = kpe
        vT_ref[0, hd, 0, 0:MLA_V, :] = vT[hd * MLA_V:(hd + 1) * MLA_V].astype(BF16)
        vT_ref[0, hd, 0, MLA_V:MLA_V + SUM_ROWS, :] = jnp.ones((SUM_ROWS, tm), BF16)

    _mem_attention(qm, mkT_ref, mv_ref, om_ref)


def _a_proj(x, pos_row, gain, w_in, qg, w_uqT, kvg, w_k, w_vT, mkT, mv, tm, tq):
    b, s, d = x.shape
    nt = s // tm
    r = tq // tm
    m = mkT.shape[-1]
    return pl.pallas_call(
        _a_proj_kernel,
        grid=(b, nt),
        in_specs=[
            pl.BlockSpec((1, tm, d), lambda i, t: (i, t, 0)),
            pl.BlockSpec((1, 1, tm), lambda i, t: (i, 0, t)),
            _const_spec(gain.shape),
            _const_spec(w_in.shape),
            _const_spec(qg.shape),
            _const_spec(w_uqT.shape),
            _const_spec(kvg.shape),
            _const_spec(w_k.shape),
            _const_spec(w_vT.shape),
            pl.BlockSpec((1, 1, MEM_HEADS, MEM_DIM, m), lambda i, t: (0, i, 0, 0, 0)),
            pl.BlockSpec((1, 1, MEM_HEADS, m, MEM_DIM), lambda i, t: (0, i, 0, 0, 0)),
        ],
        out_specs=[
            pl.BlockSpec((1, MLA_HEADS, 1, MLA_DK, tm), lambda i, t: (i, 0, t // r, 0, t % r)),
            pl.BlockSpec((1, MLA_HEADS, tm, MLA_DK), lambda i, t: (i, 0, t, 0)),
            pl.BlockSpec((1, MLA_HEADS, 1, MLA_V + SUM_ROWS, tm), lambda i, t: (i, 0, t, 0, 0)),
            pl.BlockSpec((1, tm, MEM_HEADS * MEM_DIM), lambda i, t: (i, t, 0)),
        ],
        out_shape=[
            jax.ShapeDtypeStruct((b, MLA_HEADS, s // tq, MLA_DK, tq), BF16),
            jax.ShapeDtypeStruct((b, MLA_HEADS, s, MLA_DK), BF16),
            jax.ShapeDtypeStruct((b, MLA_HEADS, nt, MLA_V + SUM_ROWS, tm), BF16),
            jax.ShapeDtypeStruct((b, s, MEM_HEADS * MEM_DIM), BF16),
        ],
        compiler_params=_params(2),
        name="a_proj",
    )(x, pos_row, gain, w_in, qg, w_uqT, kvg, w_k, w_vT, mkT, mv)


def _online_softmax_pv(s_ref, acc_ref, vT, cols, m_prev, masked):
    def read():
        s = s_ref[:, cols]
        if masked:
            key = lax.broadcasted_iota(jnp.int32, s.shape, 0)
            qry = lax.broadcasted_iota(jnp.int32, s.shape, 1)
            s = jnp.where(key <= qry, s, NEG_BIG)
        return s

    m_new = jnp.maximum(m_prev, jnp.max(read(), axis=0, keepdims=True))
    acc_ref[:, cols] = jnp.exp2(m_prev - m_new) * acc_ref[:, cols]
    p = jnp.exp2(read() - m_new).astype(BF16)
    acc_ref[:, cols] += _dot(vT, p)
    return m_new


def _fold_key_block(s_ref, acc_ref, vT, m_prev, q_lo):
    if q_lo is None:
        qc = min(QUERY_CHUNK, s_ref.shape[1])
        parts = []
        for c0 in range(0, s_ref.shape[1], qc):
            cols = slice(c0, c0 + qc)
            parts.append(_online_softmax_pv(s_ref, acc_ref, vT, cols, m_prev[:, cols], False))
        return jnp.concatenate(parts, axis=1)
    m_part = _online_softmax_pv(s_ref, acc_ref, vT, slice(q_lo, None), m_prev[:, q_lo:], True)
    return m_part if q_lo == 0 else jnp.concatenate([m_prev[:, :q_lo], m_part], axis=1)


QUERY_CHUNK = 512
MLA_KEY_UNROLL = 8
DIFF_KEY_UNROLL = 8


def _pipelined_key_blocks(qi, nd, tk, unroll, scores, consume, finish, init):
    n_full = qi * nd
    scores(0, 0, None)

    def group(g, carry):
        j = unroll * g
        for u in range(unroll):
            scores(j + u + 1, (u + 1) % 2, None)
            carry = consume(u % 2, j + u, carry, None)
        return carry

    carry = lax.fori_loop(0, n_full // unroll, group, init)
    base = (n_full // unroll) * unroll

    for rem in range(0, unroll, math.gcd(nd, unroll)):
        @pl.when(n_full % unroll == rem)
        def _(rem=rem):
            blocks = [(base + u, None) for u in range(rem)] + [(n_full + d, d * tk) for d in range(nd)]
            c = carry
            for idx, (j, q_lo) in enumerate(blocks):
                if idx + 1 < len(blocks):
                    scores(blocks[idx + 1][0], (idx + 1) % 2, blocks[idx + 1][1])
                c = consume(idx % 2, j, c, q_lo)
            finish(c)


def _mla_attn_kernel(qT_ref, k_ref, vT_ref, o_ref, sa_ref, sb_ref, acc_ref):
    tq = qT_ref.shape[4]
    tk = vT_ref.shape[4]
    qi = pl.program_id(2)
    qT = qT_ref[0, 0, 0]
    s_refs = (sa_ref, sb_ref)

    def scores(j, slot, q_lo):
        off = pl.multiple_of(j * tk, tk)
        lo = q_lo or 0
        s_refs[slot][:, lo:] = _dot(k_ref[0, 0, pl.ds(off, tk), :], qT[:, lo:])

    def consume(slot, j, carry, q_lo):
        return _fold_key_block(s_refs[slot], acc_ref, vT_ref[0, 0, j], carry, q_lo)

    def finish(carry):
        l = acc_ref[MLA_V:MLA_V + 1, :]
        o_ref[0] = (acc_ref[0:MLA_V, :] / l).T.astype(o_ref.dtype)

    acc_ref[...] = jnp.zeros(acc_ref.shape, F32)
    init = jnp.full((1, tq), NEG_BIG, F32)
    _pipelined_key_blocks(qi, tq // tk, tk, MLA_KEY_UNROLL, scores, consume, finish, init)


def _mla_attn(qT, k, vT):
    b, nh, nq, _, tq = qT.shape
    nkb, tk = vT.shape[2], vT.shape[4]
    s = nq * tq
    return pl.pallas_call(
        _mla_attn_kernel,
        grid=(b, nh, nq),
        in_specs=[
            pl.BlockSpec((1, 1, 1, MLA_DK, tq), lambda i, h, t: (i, h, t, 0, 0)),
            pl.BlockSpec((1, 1, s, MLA_DK), lambda i, h, t: (i, h, 0, 0)),
            pl.BlockSpec((1, 1, nkb, MLA_V + SUM_ROWS, tk), lambda i, h, t: (i, h, 0, 0, 0)),
        ],
        out_specs=pl.BlockSpec((1, tq, MLA_V), lambda i, h, t: (i, t, h)),
        out_shape=jax.ShapeDtypeStruct((b, s, nh * MLA_V), BF16),
        scratch_shapes=[pltpu.VMEM((tk, tq), F32), pltpu.VMEM((tk, tq), F32),
                        pltpu.VMEM((MLA_V + SUM_ROWS, tq), F32)],
        compiler_params=_params(3),
        name="mla_attn",
    )(qT, k, vT)


def _diff_attn_kernel(lam_init, q1T_ref, q2T_ref, k1_ref, k2_ref, vT_ref, lam_ref, sub_ref, o_ref,
                      s1a_ref, s1b_ref, s2a_ref, s2b_ref, acc1_ref, acc2_ref):
    tq = q1T_ref.shape[4]
    tk = vT_ref.shape[4]
    qi = pl.program_id(2)
    q1T = q1T_ref[0, 0, 0]
    q2T = q2T_ref[0, 0, 0]
    s1_refs = (s1a_ref, s1b_ref)
    s2_refs = (s2a_ref, s2b_ref)

    def scores(j, slot, q_lo):
        off = pl.multiple_of(j * tk, tk)
        lo = q_lo or 0
        s1_refs[slot][:, lo:] = _dot(k1_ref[0, 0, pl.ds(off, tk), :], q1T[:, lo:])
        s2_refs[slot][:, lo:] = _dot(k2_ref[0, 0, pl.ds(off, tk), :], q2T[:, lo:])

    def consume(slot, j, carry, q_lo):
        vT = vT_ref[0, 0, j]
        m1 = _fold_key_block(s1_refs[slot], acc1_ref, vT, carry[0], q_lo)
        m2 = _fold_key_block(s2_refs[slot], acc2_ref, vT, carry[1], q_lo)
        return m1, m2

    def finish(carry):
        l1 = acc1_ref[DIFF_V:DIFF_V + 1, :]
        l2 = acc2_ref[DIFF_V:DIFF_V + 1, :]
        lp = lam_ref[...]
        lam = (jnp.exp(jnp.sum(lp[0:1] * lp[1:2], axis=-1, keepdims=True))
               - jnp.exp(jnp.sum(lp[2:3] * lp[3:4], axis=-1, keepdims=True)) + lam_init)
        oT = acc1_ref[0:DIFF_V, :] / l1 - lam * (acc2_ref[0:DIFF_V, :] / l2)
        o_ref[0] = (_rms(oT.T, sub_ref[...]) * (1.0 - lam_init)).astype(o_ref.dtype)

    acc1_ref[...] = jnp.zeros(acc1_ref.shape, F32)
    acc2_ref[...] = jnp.zeros(acc2_ref.shape, F32)
    m_init = jnp.full((1, tq), NEG_BIG, F32)
    _pipelined_key_blocks(qi, tq // tk, tk, DIFF_KEY_UNROLL, scores, consume, finish, (m_init, m_init))


def _diff_attn(q1T, q2T, k1, k2, vT, lam_params, subln, lam_init):
    b, nh, nq, _, tq = q1T.shape
    nkb, tk = vT.shape[2], vT.shape[4]
    s = nq * tq
    qT_spec = pl.BlockSpec((1, 1, 1, DIFF_DIM, tq), lambda i, h, t: (i, h, t, 0, 0))
    k_spec = pl.BlockSpec((1, 1, s, DIFF_DIM), lambda i, h, t: (i, h, 0, 0))
    return pl.pallas_call(
        functools.partial(_diff_attn_kernel, lam_init),
        grid=(b, nh, nq),
        in_specs=[
            qT_spec, qT_spec, k_spec, k_spec,
            pl.BlockSpec((1, 1, nkb, DIFF_V + SUM_ROWS, tk), lambda i, h, t: (i, h, 0, 0, 0)),
            _const_spec(lam_params.shape),
            _const_spec(subln.shape),
        ],
        out_specs=pl.BlockSpec((1, tq, DIFF_V), lambda i, h, t: (i, t, h)),
        out_shape=jax.ShapeDtypeStruct((b, s, nh * DIFF_V), BF16),
        scratch_shapes=[pltpu.VMEM((tk, tq), F32)] * 4 + [pltpu.VMEM((DIFF_V + SUM_ROWS, tq), F32)] * 2,
        compiler_params=_params(3),
        name="diff_attn",
    )(q1T, q2T, k1, k2, vT, lam_params, subln)


def _out_proj_kernel(o_ref, om_ref, x_ref, w1_ref, w2_ref, g_ref, out_ref):
    mix = _dot(o_ref[0], w1_ref[...]) + _dot(om_ref[0], w2_ref[...])
    out_ref[0] = x_ref[0] + _rms(mix, g_ref[...])


def _out_proj(o, om, x, w1, w2, gain, tm):
    b, s, d = x.shape
    return pl.pallas_call(
        _out_proj_kernel,
        grid=(b, s // tm),
        in_specs=[
            pl.BlockSpec((1, tm, o.shape[-1]), lambda i, t: (i, t, 0)),
            pl.BlockSpec((1, tm, om.shape[-1]), lambda i, t: (i, t, 0)),
            pl.BlockSpec((1, tm, d), lambda i, t: (i, t, 0)),
            _const_spec(w1.shape),
            _const_spec(w2.shape),
            _const_spec(gain.shape),
        ],
        out_specs=pl.BlockSpec((1, tm, d), lambda i, t: (i, t, 0)),
        out_shape=jax.ShapeDtypeStruct((b, s, d), F32),
        compiler_params=_params(2),
        name="out_proj",
    )(o, om, x, w1, w2, gain)


FFN_SUB = 256
FFN_ROWS = 256
NORM_ROWS = 32


def _gelu_tanh(g):
    c = math.sqrt(2.0 / math.pi)
    return 0.5 * g * (1.0 + jnp.tanh(c * (g + 0.044715 * (g * g * g))))


def _causal_conv(u_ref, r, cw, cb):
    base = SUBLANES + r
    u0 = u_ref[base:base + FFN_ROWS, :]
    u1 = u_ref[base - 1:base - 1 + FFN_ROWS, :]
    u2 = u_ref[base - 2:base - 2 + FFN_ROWS, :]
    return cb + cw[0:1] * u2 + cw[1:2] * u1 + cw[2:3] * u0


def _conv_ffn_kernel(x_ref, pre_ref, wg_ref, wv_ref, cwg_ref, cwv_ref, cbg_ref, cbv_ref, wo_ref,
                     post_ref, out_ref, h_ref, acc_ref, tail_ref, u_ref):
    t = pl.program_id(1)
    c = pl.program_id(2)
    nc = pl.num_programs(2)
    tm = x_ref.shape[1]
    fc = wg_ref.shape[1]

    @pl.when(c == 0)
    def _():
        h_ref[...] = _rms(x_ref[0], pre_ref[...]).astype(BF16)
        acc_ref[...] = jnp.zeros(acc_ref.shape, F32)

    @pl.when(t == 0)
    def _():
        tail_ref[c] = jnp.zeros(tail_ref.shape[1:], F32)

    subs = list(range(0, fc, FFN_SUB))
    for si, lo in enumerate(subs):
        u_ref[si, 0, 0:SUBLANES, :] = tail_ref[c, 0, :, lo:lo + FFN_SUB]
        u_ref[si, 1, 0:SUBLANES, :] = tail_ref[c, 1, :, lo:lo + FFN_SUB]

    groups = [(si, r) for si in range(len(subs)) for r in range(0, tm, FFN_ROWS)]

    def up(si, r):
        lo = subs[si]
        hr = h_ref[r:r + FFN_ROWS, :]
        rows = slice(SUBLANES + r, SUBLANES + r + FFN_ROWS)
        u_ref[si, 0, rows, :] = _dot(hr, wg_ref[:, lo:lo + FFN_SUB])
        u_ref[si, 1, rows, :] = _dot(hr, wv_ref[:, lo:lo + FFN_SUB])

    for g in groups:
        up(*g)
    ys = []
    for si, r in groups:
        lo, hi = subs[si], subs[si] + FFN_SUB
        cg = _causal_conv(u_ref.at[si, 0], r, cwg_ref[:, lo:hi], cbg_ref[:, lo:hi])
        cv = _causal_conv(u_ref.at[si, 1], r, cwv_ref[:, lo:hi], cbv_ref[:, lo:hi])
        ys.append((_gelu_tanh(cg) * cv).astype(BF16))
    for (si, r), y in zip(groups, ys):
        lo, hi = subs[si], subs[si] + FFN_SUB
        acc_ref[r:r + FFN_ROWS, :] += _dot(y, wo_ref[lo:hi, :])

    for si, lo in enumerate(subs):
        tail_ref[c, 0, :, lo:lo + FFN_SUB] = u_ref[si, 0, tm:tm + SUBLANES, :]
        tail_ref[c, 1, :, lo:lo + FFN_SUB] = u_ref[si, 1, tm:tm + SUBLANES, :]

    @pl.when(c == nc - 1)
    def _():
        for r in range(0, tm, NORM_ROWS):
            rows = slice(r, r + NORM_ROWS)
            out_ref[0, rows, :] = x_ref[0, rows, :] + _rms(acc_ref[rows, :], post_ref[...])


def _conv_ffn(x, pre, w_in, conv_w, conv_b, w_out, post, tm, fc):
    b, s, d = x.shape
    nc = D_FF // fc
    return pl.pallas_call(
        _conv_ffn_kernel,
        grid=(b, s // tm, nc),
        in_specs=[
            pl.BlockSpec((1, tm, d), lambda i, t, c: (i, t, 0)),
            _const_spec(pre.shape),
            pl.BlockSpec((d, fc), lambda i, t, c: (0, c)),
            pl.BlockSpec((d, fc), lambda i, t, c: (0, c + nc)),
            pl.BlockSpec((CONV_WIDTH, fc), lambda i, t, c: (0, c)),
            pl.BlockSpec((CONV_WIDTH, fc), lambda i, t, c: (0, c + nc)),
            pl.BlockSpec((1, fc), lambda i, t, c: (0, c)),
            pl.BlockSpec((1, fc), lambda i, t, c: (0, c + nc)),
            pl.BlockSpec((fc, d), lambda i, t, c: (c, 0)),
            _const_spec(post.shape),
        ],
        out_specs=pl.BlockSpec((1, tm, d), lambda i, t, c: (i, t, 0)),
        out_shape=jax.ShapeDtypeStruct((b, s, d), F32),
        scratch_shapes=[
            pltpu.VMEM((tm, d), BF16),
            pltpu.VMEM((tm, d), F32),
            pltpu.VMEM((nc, 2, SUBLANES, fc), F32),
            pltpu.VMEM((fc // FFN_SUB, 2, SUBLANES + tm, FFN_SUB), F32),
        ],
        compiler_params=_params(3),
        name="conv_ffn",
    )(x, pre, w_in, w_in, conv_w, conv_w, conv_b, conv_b, w_out, post)


def _b_proj_kernel(x_ref, pos_ref, gs_ref, gb_ref, w_k_ref, w_vT_ref, w_qT_ref, w_qm_ref,
                   mkT_ref, mv_ref, q1T_ref, q2T_ref, k1_ref, k2_ref, vT_ref, om_ref):
    xf = x_ref[0]
    n = xf * lax.rsqrt(jnp.mean(xf * xf, axis=-1, keepdims=True) + NORM_EPS)
    hs = (n * gs_ref[...]).astype(BF16)
    hb = (n * gb_ref[...]).astype(BF16)

    half = DIFF_DIM // 2
    cos_r, sin_r = _rope_tables_rows(pos_ref[0], half, DIFF_DIM)

    tab_c = jnp.concatenate([cos_r, cos_r], axis=0).T
    tab_s = jnp.concatenate([-sin_r, sin_r], axis=0).T
    kk = _dot(hs, w_k_ref[...])
    vT = _dot_nt(w_vT_ref[...], hs)
    nk = DIFF_HEADS * DIFF_DIM
    for hd in range(DIFF_HEADS):
        for which, ref in ((0, k1_ref), (1, k2_ref)):
            base = which * nk + hd * DIFF_DIM
            xr = kk[:, base:base + DIFF_DIM]
            ref[0, hd] = (xr * tab_c + pltpu.roll(xr, half, 1) * tab_s).astype(BF16)
        vT_ref[0, hd, 0, 0:DIFF_V, :] = vT[hd * DIFF_V:(hd + 1) * DIFF_V].astype(BF16)
        vT_ref[0, hd, 0, DIFF_V:DIFF_V + SUM_ROWS, :] = jnp.ones((SUM_ROWS, x_ref.shape[1]), BF16)

    scale = DIFF_DIM ** -0.5 * LOG2_E
    cos_q, sin_q = cos_r * scale, sin_r * scale
    qT = _dot_nt(w_qT_ref[...], hb)
    for hd in range(DIFF_HEADS):
        for which, ref in ((0, q1T_ref), (1, q2T_ref)):
            base = hd * 2 * DIFF_DIM + which * DIFF_DIM
            r1, r2 = _rope_rows(qT[base:base + half], qT[base + half:base + DIFF_DIM], cos_q, sin_q)
            ref[0, hd, 0] = jnp.concatenate([r1, r2], axis=0).astype(BF16)

    _mem_attention(_dot(hb, w_qm_ref[...]), mkT_ref, mv_ref, om_ref)


def _b_proj(x, pos_row, gs, gb, w_k, w_vT, w_qT, w_qm, mkT, mv, tm, tq):
    b, s, d = x.shape
    nt = s // tm
    r = tq // tm
    m = mkT.shape[-1]
    qT_shape = jax.ShapeDtypeStruct((b, DIFF_HEADS, s // tq, DIFF_DIM, tq), BF16)
    k_shape = jax.ShapeDtypeStruct((b, DIFF_HEADS, s, DIFF_DIM), BF16)
    qT_spec = pl.BlockSpec((1, DIFF_HEADS, 1, DIFF_DIM, tm), lambda i, t: (i, 0, t // r, 0, t % r))
    k_spec = pl.BlockSpec((1, DIFF_HEADS, tm, DIFF_DIM), lambda i, t: (i, 0, t, 0))
    return pl.pallas_call(
        _b_proj_kernel,
        grid=(b, nt),
        in_specs=[
            pl.BlockSpec((1, tm, d), lambda i, t: (i, t, 0)),
            pl.BlockSpec((1, 1, tm), lambda i, t: (i, 0, t)),
            _const_spec(gs.shape),
            _const_spec(gb.shape),
            _const_spec(w_k.shape),
            _const_spec(w_vT.shape),
            _const_spec(w_qT.shape),
            _const_spec(w_qm.shape),
            pl.BlockSpec((1, 1, MEM_HEADS, MEM_DIM, m), lambda i, t: (1, i, 0, 0, 0)),
            pl.BlockSpec((1, 1, MEM_HEADS, m, MEM_DIM), lambda i, t: (1, i, 0, 0, 0)),
        ],
        out_specs=[
            qT_spec, qT_spec, k_spec, k_spec,
            pl.BlockSpec((1, DIFF_HEADS, 1, DIFF_V + SUM_ROWS, tm), lambda i, t: (i, 0, t, 0, 0)),
            pl.BlockSpec((1, tm, MEM_HEADS * MEM_DIM), lambda i, t: (i, t, 0)),
        ],
        out_shape=[
            qT_shape, qT_shape, k_shape, k_shape,
            jax.ShapeDtypeStruct((b, DIFF_HEADS, nt, DIFF_V + SUM_ROWS, tm), BF16),
            jax.ShapeDtypeStruct((b, s, MEM_HEADS * MEM_DIM), BF16),
        ],
        compiler_params=_params(2),
        name="b_proj",
    )(x, pos_row, gs, gb, w_k, w_vT, w_qT, w_qm, mkT, mv)


def _row(v):
    return v.reshape(1, -1).astype(F32)


def kernel(x, mem, positions, a_attn_norm_pre, a_w_in, a_q_norm, a_w_uq, a_kv_norm, a_w_ukv, a_mem_norm, a_w_mem_kv, a_w_o, a_attn_norm_post, shared_kv_norm, shared_w_kv, b_attn_norm_pre, b_w_in, b_lambda_q1, b_lambda_k1, b_lambda_q2, b_lambda_k2, b_subln, b_mem_norm, b_w_mem_kv, b_w_o, b_attn_norm_post, ffn_norm_pre, ffn_w_in, ffn_conv_w, ffn_conv_b, ffn_w_out, ffn_norm_post):
    b, s, d = x.shape
    assert d == D_MODEL and a_attn_norm_pre.shape[0] == 1 and b_attn_norm_pre.shape[0] == 1
    tm, tq_mla, tq_diff, fc = _tiles(s)
    assert all(s % tq == 0 and tq % tm == 0 for tq in (tq_mla, tq_diff)) and D_FF % fc == 0
    pos_row = positions.reshape(b, 1, s)

    mem_gains = jnp.stack([a_mem_norm[0], b_mem_norm[0]]).reshape(2, 1, d)
    mem_w = jnp.stack([a_w_mem_kv[0], b_w_mem_kv[0]]).astype(BF16)
    mkT, mv = _mem_kv(mem, mem_gains, mem_w)

    w_in = a_w_in[0]
    i2 = Q_LORA + KV_LORA + MLA_ROPE
    w_in_p = jnp.concatenate(
        [w_in[:, :i2], jnp.zeros((d, LANES - MLA_ROPE), F32), w_in[:, i2:]], axis=1).astype(BF16)
    w_uqT = a_w_uq[0].T.astype(BF16)
    w_ukv = a_w_ukv[0].reshape(KV_LORA, MLA_HEADS, MLA_NOPE + MLA_V)
    w_k = w_ukv[:, :, :MLA_NOPE].reshape(KV_LORA, MLA_HEADS * MLA_NOPE).astype(BF16)
    w_vT = w_ukv[:, :, MLA_NOPE:].reshape(KV_LORA, MLA_HEADS * MLA_V).T.astype(BF16)
    qT, k, vT, om = _a_proj(x, pos_row, _row(a_attn_norm_pre[0]), w_in_p, _row(a_q_norm[0]), w_uqT,
                            _row(a_kv_norm[0]), w_k, w_vT, mkT, mv, tm, tq_mla)
    o = _mla_attn(qT, k, vT)
    n_main = MLA_HEADS * MLA_V
    w_o = a_w_o[0].astype(BF16)
    x = _out_proj(o, om, x, w_o[:n_main], w_o[n_main:], _row(a_attn_norm_post[0]), tm)
    x = _conv_ffn(x, _row(ffn_norm_pre[0]), ffn_w_in[0].astype(BF16), ffn_conv_w[0],
                  _row(ffn_conv_b[0]), ffn_w_out[0].astype(BF16), _row(ffn_norm_post[0]), tm, fc)

    layer = N_A_LAYERS
    lam_init = 0.8 - 0.6 * math.exp(-0.3 * layer)
    w_kv = shared_w_kv.reshape(d, DIFF_HEADS, 2 * DIFF_DIM + DIFF_V)
    nk = DIFF_HEADS * DIFF_DIM
    w_k12 = jnp.concatenate(
        [w_kv[:, :, :DIFF_DIM].reshape(d, nk), w_kv[:, :, DIFF_DIM:2 * DIFF_DIM].reshape(d, nk)],
        axis=1).astype(BF16)
    w_svT = w_kv[:, :, 2 * DIFF_DIM:].reshape(d, DIFF_HEADS * DIFF_V).T.astype(BF16)
    nq_cols = DIFF_HEADS * 2 * DIFF_DIM
    w_bq_T = b_w_in[0][:, :nq_cols].T.astype(BF16)
    w_bqm = b_w_in[0][:, nq_cols:].astype(BF16)
    q1T, q2T, k1, k2, vbT, omb = _b_proj(x, pos_row, _row(shared_kv_norm), _row(b_attn_norm_pre[0]),
                                         w_k12, w_svT, w_bq_T, w_bqm, mkT, mv, tm, tq_diff)
    lam_params = jnp.stack([b_lambda_q1[0], b_lambda_k1[0], b_lambda_q2[0], b_lambda_k2[0]]).astype(F32)
    ob = _diff_attn(q1T, q2T, k1, k2, vbT, lam_params, _row(b_subln[0]), lam_init)
    n_main_b = DIFF_HEADS * DIFF_V
    w_ob = b_w_o[0].astype(BF16)
    x = _out_proj(ob, omb, x, w_ob[:n_main_b], w_ob[n_main_b:], _row(b_attn_norm_post[0]), tm)
    x = _conv_ffn(x, _row(ffn_norm_pre[1]), ffn_w_in[1].astype(BF16), ffn_conv_w[1],
                  _row(ffn_conv_b[1]), ffn_w_out[1].astype(BF16), _row(ffn_norm_post[1]), tm, fc)
    return x
```
